```python
import math
import jax
import jax.numpy as jnp
from jax import lax
import numpy as np

D_MODEL = 2048
BATCH = 1
SEQ = 16384
DEPTH = 4

GRID_W = 64
CTX_LEN = 256
N_MIXERS = 2
N_SSD = (DEPTH + 1) // 2
N_FNET = DEPTH // 2
EXPAND = 2
D_INNER = EXPAND * D_MODEL
HEAD_DIM = 64
N_HEADS = D_INNER // HEAD_DIM
N_GROUPS = 8
HEADS_PER_GROUP = N_HEADS // N_GROUPS
D_STATE = 128
CONV_K = 3
CONV_DIM = D_INNER + 2 * N_GROUPS * D_STATE
IN_DIM = 2 * D_INNER + 2 * N_GROUPS * D_STATE + 2 * N_HEADS
CHUNK = 128
DT_MIN = 1e-3
DT_MAX = 1e-1
FNET_GROUPS = 8
N_EXPERTS = 16
EXPERT_FF = D_MODEL // 2
CAPACITY_FACTOR = 2
EPS = 1e-6

kernel_name = "hybrid_ssd_fnet_ecmoe_prefix_dit"


def rms_norm(x, w):
    xf = x.astype(jnp.float32)
    y = xf * lax.rsqrt(jnp.mean(xf * xf, axis=-1, keepdims=True) + EPS)
    return y.astype(x.dtype) * w


def ada_params(cond, w, b):
    return jnp.split(jax.nn.silu(cond) @ w + b, 6, axis=-1)


def modulate(x, gain, shift, scale):
    return rms_norm(x, gain) * (1 + scale) + shift


def dwconv_grid(u, w, bias, rows):
    b, seq, ch = u.shape
    img = u.reshape(b, rows, GRID_W, ch)
    out = lax.conv_general_dilated(img, w[:, :, None, :], window_strides=(1, 1), padding='SAME',
                                   dimension_numbers=('NHWC', 'HWIO', 'NHWC'), feature_group_count=ch)
    return out.reshape(b, seq, ch) + bias


def dwconv_seq(u, w, bias):
    out = lax.conv_general_dilated(u, w[:, None, :], window_strides=(1,), padding='SAME',
                                   dimension_numbers=('NWC', 'WIO', 'NWC'), feature_group_count=u.shape[-1])
    return out + bias


def ssd_scan(x, dt, a, bm, cm, init):
    f32 = jnp.float32
    b, seq = x.shape[:2]
    nc = seq // CHUNK
    xdt = (x.astype(f32) * dt[..., None]).reshape(b, seq, N_GROUPS, HEADS_PER_GROUP, HEAD_DIM)
    da = (dt * a).reshape(b, seq, N_GROUPS, HEADS_PER_GROUP)

    def chunks(t):
        return jnp.moveaxis(t.reshape((b, nc, CHUNK) + t.shape[2:]), 1, 0)

    inputs = (chunks(xdt), chunks(da), chunks(bm.astype(f32)), chunks(cm.astype(f32)))
    lower = jnp.tril(jnp.ones((CHUNK, CHUNK), dtype=bool))[None, :, :, None, None]

    def step(state, inp):
        xc, ac, bc, cc = inp
        acs = jnp.cumsum(ac, axis=1)
        seg = acs[:, :, None] - acs[:, None, :]
        decay = jnp.exp(jnp.where(lower, seg, -jnp.inf))
        cb = jnp.einsum('blgn,bsgn->blsg', cc, bc)
        y = jnp.einsum('blsg,blsgk,bsgkp->blgkp', cb, decay, xc)
        y = y + jnp.einsum('blgn,bgkpn->blgkp', cc, state) * jnp.exp(acs)[..., None]
        last = acs[:, -1]
        w_in = jnp.exp(last[:, None] - acs)
        state = state * jnp.exp(last)[..., None, None] + jnp.einsum('bsgn,bsgk,bsgkp->bgkpn', bc, w_in, xc)
        return state, y

    init = init.reshape(b, N_GROUPS, HEADS_PER_GROUP, HEAD_DIM, D_STATE)
    final, ys = lax.scan(step, init, inputs)
    y = jnp.moveaxis(ys, 0, 1).reshape(b, seq, N_HEADS, HEAD_DIM)
    return y, final.reshape(b, N_HEADS, HEAD_DIM, D_STATE)


def ssd_project(h, w_in, conv):
    b, seq, _ = h.shape
    zxbcdt = h @ w_in
    z, xbc, dt_raw = jnp.split(zxbcdt, [D_INNER, D_INNER + CONV_DIM], axis=-1)
    xbc = jax.nn.silu(conv(xbc))
    xs, bm, cm = jnp.split(xbc, [D_INNER, D_INNER + N_GROUPS * D_STATE], axis=-1)
    return (z, xs.reshape(b, seq, N_HEADS, HEAD_DIM), bm.reshape(b, seq, N_GROUPS, D_STATE),
            cm.reshape(b, seq, N_GROUPS, D_STATE), dt_raw.reshape(b, seq, 2, N_HEADS))


def ssd_output(y_f, y_b, xs, z, d_skip, norm_w, w_out):
    f32 = jnp.float32
    b, seq = z.shape[:2]
    y = y_f + y_b + xs.astype(f32) * d_skip.astype(f32)[:, None]
    y = y.reshape(b, seq, D_INNER) * jax.nn.silu(z.astype(f32))
    yg = y.reshape(b, seq, N_GROUPS, D_INNER // N_GROUPS)
    yg = yg * lax.rsqrt(jnp.mean(yg * yg, axis=-1, keepdims=True) + EPS)
    y = yg.reshape(b, seq, D_INNER).astype(z.dtype) * norm_w
    return y @ w_out


def ssd_mixer(h_lat, h_ctx, rows, w_in, conv_w, conv_b, dt_bias, a_log, d_skip, norm_w, w_out, ctx_out):
    f32 = jnp.float32
    z_c, x_c, b_c, c_c, dt_c = ssd_project(h_ctx, w_in, lambda u: dwconv_seq(u, conv_w[CONV_K // 2], conv_b))
    z_l, x_l, b_l, c_l, dt_l = ssd_project(h_lat, w_in, lambda u: dwconv_grid(u, conv_w, conv_b, rows))
    a = -jnp.exp(a_log.astype(f32))
    dt_c = jax.nn.softplus(dt_c.astype(f32) + dt_bias.astype(f32))
    dt_l = jax.nn.softplus(dt_l.astype(f32) + dt_bias.astype(f32))
    zero = jnp.zeros((h_lat.shape[0], N_HEADS, HEAD_DIM, D_STATE), f32)
    rev = lambda t: jnp.flip(t, axis=1)
    yc_f, sc_f = ssd_scan(x_c, dt_c[:, :, 0], a[0], b_c, c_c, zero)
    yl_f, _ = ssd_scan(x_l, dt_l[:, :, 0], a[0], b_l, c_l, sc_f)
    yc_b, sc_b = ssd_scan(rev(x_c), rev(dt_c[:, :, 1]), a[1], rev(b_c), rev(c_c), zero)
    yl_b, _ = ssd_scan(rev(x_l), rev(dt_l[:, :, 1]), a[1], rev(b_l), rev(c_l), sc_b)
    out_lat = ssd_output(yl_f, rev(yl_b), x_l, z_l, d_skip, norm_w, w_out)
    out_ctx = ssd_output(yc_f, rev(yc_b), x_c, z_c, d_skip, norm_w, w_out) if ctx_out else None
    return out_lat, out_ctx


def fourier_mixer(h, w_out):
    b, seq, d = h.shape
    hg = h.astype(jnp.float32).reshape(b, seq, FNET_GROUPS, d // FNET_GROUPS)
    f = jnp.fft.fft2(hg, axes=(1, 3), norm='ortho').real
    return f.reshape(b, seq, d).astype(h.dtype) @ w_out


def ec_moe(h, w_router, w_gate, w_up, w_down):
    n = h.shape[1]
    cap = (CAPACITY_FACTOR * n) // N_EXPERTS
    aff = jax.nn.softmax(jnp.einsum('bnd,de->bne', h, w_router).astype(jnp.float32), axis=-1)
    g, idx = lax.top_k(jnp.swapaxes(aff, 1, 2), cap)

    def per_sample(h_b, g_b, idx_b):
        xs = h_b[idx_b]
        a = jnp.einsum('ecd,edf->ecf', xs, w_gate)
        u = jnp.einsum('ecd,edf->ecf', xs, w_up)
        y = jnp.einsum('ecf,efd->ecd', jax.nn.silu(a) * u, w_down) * g_b[..., None].astype(h_b.dtype)
        return jnp.zeros_like(h_b).at[idx_b.reshape(-1)].add(y.reshape(-1, h_b.shape[-1]))

    return jax.vmap(per_sample)(h, g, idx)


def setup_inputs(seed: int = 0) -> dict:
    key = jax.random.key(seed)
    ks = jax.random.split(key, 24)
    f32 = jnp.float32
    nrm = lambda k, shape, s: jax.random.normal(k, shape, f32) * s
    x = nrm(ks[0], (BATCH, SEQ, D_MODEL), 1.0)
    c = nrm(ks[1], (BATCH, D_MODEL), 1.0)
    ctx = nrm(ks[2], (BATCH, CTX_LEN, D_MODEL), 1.0)
    c_ctx = nrm(ks[3], (D_MODEL,), 1.0)
    ada_w = nrm(ks[4], (DEPTH, D_MODEL, 6 * D_MODEL), 0.5 * D_MODEL ** -0.5)
    ada_b = nrm(ks[5], (DEPTH, 6 * D_MODEL), 0.02)
    norm1_w = 1.0 + nrm(ks[6], (DEPTH, D_MODEL), 0.05)
    norm2_w = 1.0 + nrm(ks[7], (DEPTH, D_MODEL), 0.05)
    final_norm_w = 1.0 + nrm(ks[8], (D_MODEL,), 0.05)
    ssd_w_in = nrm(ks[9], (N_SSD, D_MODEL, IN_DIM), D_MODEL ** -0.5)
    ssd_conv_w = nrm(ks[10], (N_SSD, CONV_K, CONV_K, CONV_DIM), 1.0 / CONV_K)
    ssd_conv_b = nrm(ks[11], (N_SSD, CONV_DIM), 0.02)
    dt0 = jnp.exp(jax.random.uniform(ks[12], (N_SSD, 2, N_HEADS), f32, math.log(DT_MIN), math.log(DT_MAX)))
    ssd_dt_bias = dt0 + jnp.log(-jnp.expm1(-dt0))
    ssd_a_log = jnp.log(jax.random.uniform(ks[13], (N_SSD, 2, N_HEADS), f32, 1.0, 16.0))
    ssd_d = 1.0 + nrm(ks[14], (N_SSD, N_HEADS), 0.05)
    ssd_norm_w = 1.0 + nrm(ks[15], (N_SSD, D_INNER), 0.05)
    ssd_w_out = nrm(ks[16], (N_SSD, D_INNER, D_MODEL), D_INNER ** -0.5)
    fnet_w_out = nrm(ks[17], (N_FNET, D_MODEL, D_MODEL), D_MODEL ** -0.5)
    moe_w_router = nrm(ks[18], (DEPTH, D_MODEL, N_EXPERTS), D_MODEL ** -0.5)
    moe_w_gate = nrm(ks[19], (DEPTH, N_EXPERTS, D_MODEL, EXPERT_FF), D_MODEL ** -0.5)
    moe_w_up = nrm(ks[20], (DEPTH, N_EXPERTS, D_MODEL, EXPERT_FF), D_MODEL ** -0.5)
    moe_w_down = nrm(ks[21], (DEPTH, N_EXPERTS, EXPERT_FF, D_MODEL), EXPERT_FF ** -0.5)
    return {"x": x, "c": c, "ctx": ctx, "c_ctx": c_ctx, "ada_w": ada_w, "ada_b": ada_b,
            "norm1_w": norm1_w, "norm2_w": norm2_w, "final_norm_w": final_norm_w,
            "ssd_w_in": ssd_w_in, "ssd_conv_w": ssd_conv_w, "ssd_conv_b": ssd_conv_b,
            "ssd_dt_bias": ssd_dt_bias, "ssd_a_log": ssd_a_log, "ssd_d": ssd_d,
            "ssd_norm_w": ssd_norm_w, "ssd_w_out": ssd_w_out, "fnet_w_out": fnet_w_out,
            "moe_w_router": moe_w_router, "moe_w_gate": moe_w_gate, "moe_w_up": moe_w_up,
            "moe_w_down": moe_w_down}


def reference(x, c, ctx, c_ctx, ada_w, ada_b, norm1_w, norm2_w, final_norm_w, ssd_w_in, ssd_conv_w,
              ssd_conv_b, ssd_dt_bias, ssd_a_log, ssd_d, ssd_norm_w, ssd_w_out, fnet_w_out,
              moe_w_router, moe_w_gate, moe_w_up, moe_w_down):
    rows = x.shape[1] // GRID_W
    ctx_s = ctx
    for i in range(DEPTH):
        k = i // N_MIXERS
        is_ssd = (i % N_MIXERS == 0)
        ctx_later = any(j % N_MIXERS == 0 for j in range(i + 1, DEPTH))
        sh1, sc1, g1, sh2, sc2, g2 = ada_params(c[:, None, :], ada_w[i], ada_b[i])
        h = modulate(x, norm1_w[i], sh1, sc1)
        if is_ssd or ctx_later:
            csh1, csc1, cg1, csh2, csc2, cg2 = ada_params(c_ctx[None, None, :], ada_w[i], ada_b[i])
            hc = modulate(ctx_s, norm1_w[i], csh1, csc1)
        if is_ssd:
            y, yc = ssd_mixer(h, hc, rows, ssd_w_in[k], ssd_conv_w[k], ssd_conv_b[k], ssd_dt_bias[k],
                              ssd_a_log[k], ssd_d[k], ssd_norm_w[k], ssd_w_out[k], ctx_later)
        else:
            y = fourier_mixer(h, fnet_w_out[k])
            yc = fourier_mixer(hc, fnet_w_out[k]) if ctx_later else None
        x = x + g1 * y
        x = x + g2 * ec_moe(modulate(x, norm2_w[i], sh2, sc2), moe_w_router[i], moe_w_gate[i],
                            moe_w_up[i], moe_w_down[i])
        if ctx_later:
            ctx_s = ctx_s + cg1 * yc
            ctx_s = ctx_s + cg2 * ec_moe(modulate(ctx_s, norm2_w[i], csh2, csc2), moe_w_router[i],
                                         moe_w_gate[i], moe_w_up[i], moe_w_down[i])
    return rms_norm(x, final_norm_w)
```

```python
import functools
import math

import jax
import jax.numpy as jnp
from jax import lax
from jax.experimental import pallas as pl
from jax.experimental.pallas import tpu as pltpu

F32 = jnp.float32
BF16 = jnp.bfloat16
I32 = jnp.int32
HIGHEST = lax.Precision.HIGHEST

V7X_LANES = 128
V7X_SUBLANES = 8
V7X_VMEM_BYTES = 64 * 1024 * 1024
VMEM_LIMIT = V7X_VMEM_BYTES - 8 * 1024 * 1024

GRID_W = 64
HEAD_DIM = 64
N_GROUPS = 8
D_STATE = 128
CHUNK = 128
FNET_GROUPS = 8
N_EXPERTS = 16
CAPACITY_FACTOR = 2
EPS = 1e-6
TOK_ROWS = 16


def _params(*sem):
    return pltpu.CompilerParams(dimension_semantics=sem, vmem_limit_bytes=VMEM_LIMIT)


def _rms_mod(x, gain, scale, shift):
    y = x * lax.rsqrt(jnp.mean(x * x, axis=-1, keepdims=True) + EPS)
    return y * gain * (1.0 + scale) + shift


def _silu(x):
    return x * jax.nn.sigmoid(x)


def _ada_kernel(c_ref, w_ref, b_ref, o_ref):
    w = w_ref[...]
    rows = []
    for r in range(2):
        s = _silu(c_ref[r])
        rows.append(jnp.sum(s * w, axis=0, keepdims=True) + b_ref[...])
    rows.append(jnp.zeros((V7X_SUBLANES - 2, w.shape[1]), F32))
    o_ref[...] = jnp.concatenate(rows, axis=0)


def ada_all(c, c_ctx, ada_w, ada_b):
    depth, k, n6 = ada_w.shape
    tn = n6 // 8
    cond = jnp.stack([c[0], c_ctx], axis=0)[:, :, None]
    out = pl.pallas_call(
        _ada_kernel,
        grid=(depth, n6 // tn),
        in_specs=[
            pl.BlockSpec((2, k, 1), lambda l, n: (0, 0, 0)),
            pl.BlockSpec((None, k, tn), lambda l, n: (l, 0, n)),
            pl.BlockSpec((None, 1, tn), lambda l, n: (l, 0, n)),
        ],
        out_specs=pl.BlockSpec((None, V7X_SUBLANES, tn), lambda l, n: (l, 0, n)),
        out_shape=jax.ShapeDtypeStruct((depth, V7X_SUBLANES, n6), F32),
        compiler_params=_params("arbitrary", "arbitrary"),
        name="ada",
    )(cond, ada_w, ada_b[:, None, :])
    return out


def _mm_kernel(*refs, prologue, epilogue):
    it = iter(refs)
    a_ref = next(it)
    if prologue:
        gain_ref, sc_ref, sh_ref = next(it), next(it), next(it)
    w_ref = next(it)
    if epilogue:
        res_ref, gate_ref = next(it), next(it)
    o_ref = next(it)
    a_scr = next(it, None)

    if a_scr is None:
        a_bf16 = a_ref[...]
    else:
        @pl.when(pl.program_id(1) == 0)
        def _():
            a = a_ref[...]
            if prologue:
                a = _rms_mod(a, gain_ref[...], sc_ref[...], sh_ref[...])
            a_scr[...] = a.astype(BF16)

        a_bf16 = a_scr[...]
    acc = jnp.dot(a_bf16, w_ref[...], preferred_element_type=F32)
    if epilogue:
        acc = res_ref[...] + gate_ref[...] * acc
    o_ref[...] = acc.astype(o_ref.dtype)


def matmul(a, w_bf16, *, mod=None, res=None, gate=None, tm, tn, out_dtype=F32, name="mm"):
    m, k = a.shape
    n = w_bf16.shape[1]
    assert m % tm == 0 and n % tn == 0
    use_scr = mod is not None or a.dtype != BF16
    row = lambda i, j: (0, 0)
    ins, specs = [a], [pl.BlockSpec((tm, k), lambda i, j: (i, 0))]
    if mod is not None:
        for v in mod:
            ins.append(v.reshape(1, k))
            specs.append(pl.BlockSpec((1, k), row))
    ins.append(w_bf16)
    specs.append(pl.BlockSpec((k, tn), lambda i, j: (0, j)))
    if res is not None:
        ins += [res, gate.reshape(1, n)]
        specs += [pl.BlockSpec((tm, tn), lambda i, j: (i, j)), pl.BlockSpec((1, tn), lambda i, j: (0, j))]
    return pl.pallas_call(
        functools.partial(_mm_kernel, prologue=mod is not None, epilogue=res is not None),
        grid=(m // tm, n // tn),
        in_specs=specs,
        out_specs=pl.BlockSpec((tm, tn), lambda i, j: (i, j)),
        out_shape=jax.ShapeDtypeStruct((m, n), out_dtype),
        scratch_shapes=[pltpu.VMEM((tm, k), BF16)] if use_scr else [],
        compiler_params=_params("arbitrary", "arbitrary"),
        name=name,
    )(*ins)


def _conv_kernel(prev_ref, cur_ref, next_ref, w_ref, b_ref, o_ref, *, width):
    i = pl.program_id(0)
    last = pl.num_programs(0) - 1
    cur = cur_ref[...]
    tb, cb = cur.shape
    prev = jnp.where(i > 0, prev_ref[...], 0.0)
    nxt = jnp.where(i < last, next_ref[...], 0.0)
    if tb > width:
        up = jnp.concatenate([prev, cur[: tb - width]], axis=0)
        dn = jnp.concatenate([cur[width:], nxt], axis=0)
    else:
        up, dn = prev, nxt
    w = w_ref[...]

    def tap(dw):
        return up * w[dw:dw + 1] + cur * w[3 + dw:4 + dw] + dn * w[6 + dw:7 + dw]

    col = lax.broadcasted_iota(I32, (tb, cb), 0) & (width - 1)
    left = jnp.where(col == 0, 0.0, pltpu.roll(tap(0), 1, axis=0))
    right = jnp.where(col == width - 1, 0.0, pltpu.roll(tap(2), tb - 1, axis=0))
    o_ref[...] = _silu(left + tap(1) + right + b_ref[...])


def conv_silu(zx, conv_w, conv_b, *, width, col0, tb, cb=512):
    n = zx.shape[0]
    c = conv_w.shape[-1]
    assert width & (width - 1) == 0 and tb % width == 0 and n % tb == 0 and c % cb == 0 and col0 % cb == 0
    cblk0 = col0 // cb
    per = tb // width
    nrow = n // width
    return pl.pallas_call(
        functools.partial(_conv_kernel, width=width),
        grid=(n // tb, c // cb),
        in_specs=[
            pl.BlockSpec((width, cb), lambda i, j: (jnp.maximum(i * per - 1, 0), cblk0 + j)),
            pl.BlockSpec((tb, cb), lambda i, j: (i, cblk0 + j)),
            pl.BlockSpec((width, cb), lambda i, j: (jnp.minimum((i + 1) * per, nrow - 1), cblk0 + j)),
            pl.BlockSpec((9, cb), lambda i, j: (0, j)),
            pl.BlockSpec((1, cb), lambda i, j: (0, j)),
        ],
        out_specs=pl.BlockSpec((tb, cb), lambda i, j: (i, j)),
        out_shape=jax.ShapeDtypeStruct((n, c), F32),
        compiler_params=_params("arbitrary", "arbitrary"),
        name="conv",
    )(zx, zx, zx, conv_w.reshape(9, c), conv_b.reshape(1, c))


def _dt_kernel(x_ref, b_ref, o_ref):
    v = x_ref[...] + b_ref[...]
    sp = jnp.maximum(v, 0.0) + jnp.log1p(jnp.exp(-jnp.abs(v)))
    o_ref[...] = sp.T


def dt_rows(zx, dt_bias, *, col0, tb):
    n = zx.shape[0]
    w = dt_bias.size
    assert col0 % w == 0 and n % tb == 0
    return pl.pallas_call(
        _dt_kernel,
        grid=(n // tb,),
        in_specs=[pl.BlockSpec((tb, w), lambda i: (i, col0 // w)), pl.BlockSpec((1, w), lambda i: (0, 0))],
        out_specs=pl.BlockSpec((w, tb), lambda i: (0, i)),
        out_shape=jax.ShapeDtypeStruct((w, n), F32),
        compiler_params=_params("arbitrary"),
        name="dt",
    )(zx, dt_bias.reshape(1, w))


def _scan_kernel(x_ref, b_ref, c_ref, dt_ref, a_ref, s0_ref, *rest, rev, has_prev):
    if has_prev:
        yp_ref, y_ref, sf_ref, st, rows = rest
    else:
        y_ref, sf_ref, st, rows = rest
    step = pl.program_id(0)
    q = CHUNK
    nh = dt_ref.shape[0]
    hpg = nh // N_GROUPS
    half = V7X_LANES // 2

    @pl.when(step == 0)
    def _():
        st[...] = s0_ref[...]

    ri = lax.broadcasted_iota(I32, (q, q), 0)
    ci = lax.broadcasted_iota(I32, (q, q), 1)
    if rev:
        upper, keep = ri >= ci, ri <= ci
    else:
        upper, keep = ri <= ci, ri >= ci
    lm = keep.astype(F32)
    dt = dt_ref[...]
    da = dt * -jnp.exp(a_ref[...])
    acs = jnp.dot(da, upper.astype(F32), precision=HIGHEST, preferred_element_type=F32)
    tot = jnp.sum(da, axis=1, keepdims=True)
    rows[0] = dt
    rows[1] = da
    rows[2] = acs
    rows[3] = dt * jnp.exp(tot - acs)
    rows[4] = jnp.broadcast_to(jnp.exp(tot), (nh, q))

    lane_lo = lax.broadcasted_iota(I32, (q, V7X_LANES), 1) < half
    r2 = lax.broadcasted_iota(I32, (2 * D_STATE, V7X_LANES), 0) < D_STATE
    l2 = lax.broadcasted_iota(I32, (2 * D_STATE, V7X_LANES), 1) < half
    diag = r2 == l2

    def group(g, carry):
        r0 = pl.multiple_of(g * hpg, hpg)
        dtg = rows[0, pl.ds(r0, hpg), :]
        dag = rows[1, pl.ds(r0, hpg), :]
        acg = rows[2, pl.ds(r0, hpg), :]
        wtg = rows[3, pl.ds(r0, hpg), :]
        etg = rows[4, pl.ds(r0, hpg), :]
        c0 = pl.multiple_of(g * D_STATE, D_STATE)
        cg = c_ref[:, pl.ds(c0, D_STATE)]
        bg = b_ref[:, pl.ds(c0, D_STATE)]
        cb = lax.dot_general(cg.astype(BF16), bg.astype(BF16), (((1,), (1,)), ((), ())),
                             preferred_element_type=F32)
        bgt = bg.T
        for k in range(hpg // 2):
            x0 = pl.multiple_of(g * (hpg * HEAD_DIM) + k * V7X_LANES, V7X_LANES)
            x2 = x_ref[:, pl.ds(x0, V7X_LANES)]
            ms, ces, bws, ess = [], [], [], []
            for t in range(2):
                j = 2 * k + t
                acol = jnp.sum(lm * dag[j:j + 1, :], axis=1, keepdims=True)
                dec = jnp.where(keep, jnp.exp(acol - acg[j:j + 1, :]), 0.0)
                ms.append(cb * dec * dtg[j:j + 1, :])
                ces.append(cg * jnp.exp(acol))
                bws.append(bgt * wtg[j:j + 1, :])
                ess.append(jnp.broadcast_to(etg[j:j + 1, :], (D_STATE, V7X_LANES)))
            sp = st[g * (hpg // 2) + k]
            lhs = jnp.concatenate(ms + ces, axis=1).astype(BF16)
            rhs = jnp.concatenate([jnp.where(lane_lo, x2, 0.0), jnp.where(lane_lo, 0.0, x2), sp],
                                  axis=0).astype(BF16)
            yp = jnp.dot(lhs, rhs, preferred_element_type=F32)
            if has_prev:
                yp = yp + yp_ref[:, pl.ds(x0, V7X_LANES)]
            y_ref[:, pl.ds(x0, V7X_LANES)] = yp
            upd = jnp.dot(jnp.concatenate(bws, axis=0).astype(BF16), x2.astype(BF16),
                          preferred_element_type=F32)
            st[g * (hpg // 2) + k] = sp * jnp.concatenate(ess, axis=0) + jnp.where(diag, upd, 0.0)
        return carry

    lax.fori_loop(0, N_GROUPS, group, 0)

    @pl.when(step == pl.num_programs(0) - 1)
    def _():
        sf_ref[...] = st[...]


def ssd_scan_dir(xbc, dt_t, a_row, s0, y_prev, *, direction):
    n = xbc.shape[0]
    nh = dt_t.shape[0] // 2
    d_inner = nh * HEAD_DIM
    gn = N_GROUPS * D_STATE
    assert n % CHUNK == 0 and d_inner % gn == 0
    nc = n // CHUNK
    rev = direction == 1
    cidx = (lambda i: nc - 1 - i) if rev else (lambda i: i)
    ins = [xbc, xbc, xbc, dt_t, a_row, s0]
    specs = [
        pl.BlockSpec((CHUNK, d_inner), lambda i: (cidx(i), 0)),
        pl.BlockSpec((CHUNK, gn), lambda i: (cidx(i), d_inner // gn)),
        pl.BlockSpec((CHUNK, gn), lambda i: (cidx(i), d_inner // gn + 1)),
        pl.BlockSpec((nh, CHUNK), lambda i: (direction, cidx(i))),
        pl.BlockSpec((nh, CHUNK), lambda i: (0, 0)),
        pl.BlockSpec(s0.shape, lambda i: (0, 0, 0)),
    ]
    if y_prev is not None:
        ins.append(y_prev)
        specs.append(pl.BlockSpec((CHUNK, d_inner), lambda i: (cidx(i), 0)))
    return pl.pallas_call(
        functools.partial(_scan_kernel, rev=rev, has_prev=y_prev is not None),
        grid=(nc,),
        in_specs=specs,
        out_specs=[pl.BlockSpec((CHUNK, d_inner), lambda i: (cidx(i), 0)),
                   pl.BlockSpec(s0.shape, lambda i: (0, 0, 0))],
        out_shape=[jax.ShapeDtypeStruct((n, d_inner), F32), jax.ShapeDtypeStruct(s0.shape, F32)],
        scratch_shapes=[pltpu.VMEM(s0.shape, F32), pltpu.VMEM((5, nh, CHUNK), F32)],
        compiler_params=_params("arbitrary"),
        name="scan_bwd" if rev else "scan_fwd",
    )(*ins)


def _gate_kernel(y_ref, xs_ref, z_ref, d_ref, nw_ref, o_ref):
    v = (y_ref[...] + xs_ref[...] * d_ref[...]) * _silu(z_ref[...])
    gw = v.shape[1] // N_GROUPS
    for g in range(N_GROUPS):
        vg = v[:, g * gw:(g + 1) * gw]
        r = lax.rsqrt(jnp.mean(vg * vg, axis=-1, keepdims=True) + EPS)
        o_ref[:, g * gw:(g + 1) * gw] = (vg * r * nw_ref[:, g * gw:(g + 1) * gw]).astype(o_ref.dtype)


def ssd_gate(y, xbc, zx, d_row, norm_w, *, tb):
    n, d_inner = y.shape
    blk = pl.BlockSpec((tb, d_inner), lambda i: (i, 0))
    vec = pl.BlockSpec((1, d_inner), lambda i: (0, 0))
    return pl.pallas_call(
        _gate_kernel,
        grid=(n // tb,),
        in_specs=[blk, blk, blk, vec, vec],
        out_specs=blk,
        out_shape=jax.ShapeDtypeStruct((n, d_inner), BF16),
        compiler_params=_params("arbitrary"),
        name="ssd_gate",
    )(y, xbc, zx, d_row.reshape(1, d_inner), norm_w.reshape(1, d_inner))


def _row_tile(n, pref):
    return pref if n % pref == 0 else n


def ssd_layer(x, ctx_s, mod_l, mod_c, norm1, w_in, conv_w, conv_b, dt_bias, a_log, d_skip, norm_w, w_out, ctx_out):
    nh = a_log.shape[1]
    d_inner = nh * HEAD_DIM
    gn = N_GROUPS * D_STATE
    w_in_b = w_in.astype(BF16)
    w_out_b = w_out.astype(BF16)
    a_rows = jnp.broadcast_to(a_log.astype(F32)[:, :, None], (2, nh, CHUNK))
    d_row = jnp.repeat(d_skip.astype(F32), HEAD_DIM)
    in_tn = w_in.shape[1] // 9

    def project(v, mod, width):
        n = v.shape[0]
        zx = matmul(v, w_in_b, mod=(norm1, mod[1], mod[0]), tm=_row_tile(n, 1024), tn=in_tn, name="ssd_in")
        xbc = conv_silu(zx, conv_w, conv_b, width=width, col0=d_inner, tb=_row_tile(n, 1024))
        dt_t = dt_rows(zx, dt_bias, col0=2 * d_inner + 2 * gn, tb=_row_tile(n, 512))
        return zx, xbc, dt_t

    zx_c, xbc_c, dt_c = project(ctx_s, mod_c, ctx_s.shape[0])
    zx_l, xbc_l, dt_l = project(x, mod_l, GRID_W)
    zero = jnp.zeros((nh // 2, 2 * D_STATE, V7X_LANES), F32)
    yc, s_f = ssd_scan_dir(xbc_c, dt_c, a_rows[0], zero, None, direction=0)
    yl, _ = ssd_scan_dir(xbc_l, dt_l, a_rows[0], s_f, None, direction=0)
    yc, s_b = ssd_scan_dir(xbc_c, dt_c, a_rows[1], zero, yc, direction=1)
    yl, _ = ssd_scan_dir(xbc_l, dt_l, a_rows[1], s_b, yl, direction=1)

    def output(v, y, xbc, zx, gate):
        n = v.shape[0]
        yn = ssd_gate(y, xbc, zx, d_row, norm_w, tb=_row_tile(n, 256))
        return matmul(yn, w_out_b, res=v, gate=gate, tm=_row_tile(n, 512), tn=1024, name="ssd_out")

    x_new = output(x, yl, xbc_l, zx_l, mod_l[2])
    ctx_new = output(ctx_s, yc, xbc_c, zx_c, mod_c[2]) if ctx_out else None
    return x_new, ctx_new


def _dft_tables(n):
    j = jnp.arange(n, dtype=I32)
    ang = ((j[:, None] * j[None, :]) % n).astype(F32) * (2.0 * math.pi / n)
    return jnp.cos(ang), jnp.sin(ang)


def _fchan_kernel(x_ref, gain_ref, sc_ref, sh_ref, cs_ref, o_ref):
    h = _rms_mod(x_ref[...], gain_ref[...], sc_ref[...], sh_ref[...]).astype(BF16)
    gw = cs_ref.shape[0]
    d = h.shape[1]
    for g in range(d // gw):
        r = jnp.dot(h[:, g * gw:(g + 1) * gw], cs_ref[...], preferred_element_type=F32)
        o_ref[:, g * gw:(g + 1) * gw] = r[:, :gw]
        o_ref[:, d + g * gw:d + (g + 1) * gw] = r[:, gw:]


def _f1_kernel(r_ref, i_ref, k1_ref, tc_ref, ts_ref, yr_ref, yi_ref):
    la, sub, cw = r_ref.shape
    rows = la * sub
    xx = jnp.concatenate([r_ref[...].reshape(rows, cw), i_ref[...].reshape(rows, cw)], axis=0).astype(BF16)
    y = jnp.dot(k1_ref[...], xx, preferred_element_type=F32)
    yr, yi = y[:rows], y[rows:]
    c, s = tc_ref[...], ts_ref[...]
    yr_ref[...] = (c * yr + s * yi).reshape(la, sub, cw)
    yi_ref[...] = (c * yi - s * yr).reshape(la, sub, cw)


def _f2_kernel(r_ref, i_ref, k2_ref, o_ref):
    sub, lb, cw = r_ref.shape
    xx = jnp.concatenate([r_ref[...].reshape(sub * lb, cw), i_ref[...].reshape(sub * lb, cw)],
                         axis=0).astype(BF16)
    o_ref[...] = jnp.dot(k2_ref[...], xx, preferred_element_type=F32).reshape(lb, sub, cw)


def fnet_layer(x, mod, norm1, w_out, *, cw=512):
    l, d = x.shape
    sub = V7X_SUBLANES
    la = l // 128 if l >= 1024 else sub
    lb = l // la
    gw = d // FNET_GROUPS
    assert la % sub == 0 and lb % sub == 0 and d % cw == 0
    shift, scale, gate = mod
    cc, sc = _dft_tables(gw)
    norm = 1.0 / math.sqrt(l * gw)
    cs = (jnp.concatenate([cc, -sc], axis=1) * norm).astype(BF16)
    tm = _row_tile(l, 512)
    vec = pl.BlockSpec((1, d), lambda i: (0, 0))
    g2 = pl.pallas_call(
        _fchan_kernel,
        grid=(l // tm,),
        in_specs=[pl.BlockSpec((tm, d), lambda i: (i, 0)), vec, vec, vec, pl.BlockSpec((gw, 2 * gw), lambda i: (0, 0))],
        out_specs=pl.BlockSpec((tm, 2 * d), lambda i: (i, 0)),
        out_shape=jax.ShapeDtypeStruct((l, 2 * d), F32),
        compiler_params=_params("arbitrary"),
        name="fnet_chan",
    )(x, norm1.reshape(1, d), scale.reshape(1, d), shift.reshape(1, d), cs)
    ca, sa = _dft_tables(la)
    eye = jnp.eye(sub, dtype=F32)
    k1 = jnp.concatenate([jnp.concatenate([jnp.kron(ca, eye), jnp.kron(sa, eye)], axis=1),
                          jnp.concatenate([jnp.kron(-sa, eye), jnp.kron(ca, eye)], axis=1)], axis=0).astype(BF16)
    bi = jnp.arange(lb, dtype=I32).reshape(lb // sub, 1, sub)
    di = jnp.arange(la, dtype=I32).reshape(1, la, 1)
    ang = ((bi * di) % l).astype(F32) * (2.0 * math.pi / l)
    tc = jnp.cos(ang).reshape(lb // sub, la * sub, 1)
    ts = jnp.sin(ang).reshape(lb // sub, la * sub, 1)
    g3 = g2.reshape(la, lb, 2 * d)
    ncb = d // cw
    blk1 = lambda off: pl.BlockSpec((la, sub, cw), lambda b, c: (0, b, off + c))
    twb = pl.BlockSpec((None, la * sub, 1), lambda b, c: (b, 0, 0))
    yr3, yi3 = pl.pallas_call(
        _f1_kernel,
        grid=(lb // sub, ncb),
        in_specs=[blk1(0), blk1(ncb), pl.BlockSpec(k1.shape, lambda b, c: (0, 0)), twb, twb],
        out_specs=[blk1(0), blk1(0)],
        out_shape=[jax.ShapeDtypeStruct((la, lb, d), F32)] * 2,
        compiler_params=_params("arbitrary", "arbitrary"),
        name="fnet_stage1",
    )(g3, g3, k1, tc, ts)
    cb_, sb_ = _dft_tables(lb)
    k2 = jnp.concatenate([jnp.einsum("cb,de->cdeb", cb_, eye).reshape(lb * sub, sub * lb),
                          jnp.einsum("cb,de->cdeb", sb_, eye).reshape(lb * sub, sub * lb)], axis=1).astype(BF16)
    blk2 = pl.BlockSpec((sub, lb, cw), lambda dd, c: (dd, 0, c))
    f3 = pl.pallas_call(
        _f2_kernel,
        grid=(la // sub, ncb),
        in_specs=[blk2, blk2, pl.BlockSpec(k2.shape, lambda dd, c: (0, 0))],
        out_specs=pl.BlockSpec((lb, sub, cw), lambda dd, c: (0, dd, c)),
        out_shape=jax.ShapeDtypeStruct((lb, la, d), F32),
        compiler_params=_params("arbitrary", "arbitrary"),
        name="fnet_stage2",
    )(yr3, yi3, k2)
    return matmul(f3.reshape(l, d), w_out.astype(BF16), res=x, gate=gate, tm=_row_tile(l, 512), tn=1024,
                  name="fnet_out")


ROUTE_S = 128
ROUTE_TOKENS = ROUTE_S * ROUTE_S


def _moe_prep_kernel(x_ref, gain_ref, sc_ref, sh_ref, wr_ref, h_ref, lg_ref):
    h = _rms_mod(x_ref[...], gain_ref[...], sc_ref[...], sh_ref[...])
    tb = h.shape[0]
    for j in range(TOK_ROWS):
        h_ref[pl.ds(j, tb, stride=TOK_ROWS), :] = h[:, j * V7X_LANES:(j + 1) * V7X_LANES]
    lg_ref[...] = lax.dot_general(wr_ref[...], h, (((1,), (1,)), ((), ())), precision=HIGHEST,
                                  preferred_element_type=F32)


def moe_prep(x, norm2, scale, shift, w_router, *, tb):
    n, d = x.shape
    e = w_router.shape[1]
    assert d == TOK_ROWS * V7X_LANES and n % tb == 0
    vec = pl.BlockSpec((1, d), lambda i: (0, 0))
    return pl.pallas_call(
        _moe_prep_kernel,
        grid=(n // tb,),
        in_specs=[pl.BlockSpec((tb, d), lambda i: (i, 0)), vec, vec, vec, pl.BlockSpec((e, d), lambda i: (0, 0))],
        out_specs=[pl.BlockSpec((tb * TOK_ROWS, V7X_LANES), lambda i: (i, 0)), pl.BlockSpec((e, tb), lambda i: (0, i))],
        out_shape=[jax.ShapeDtypeStruct((n * TOK_ROWS, V7X_LANES), F32), jax.ShapeDtypeStruct((e, n), F32)],
        compiler_params=_params("arbitrary"),
        name="moe_prep",
    )(x, norm2.reshape(1, d), scale.reshape(1, d), shift.reshape(1, d), w_router.T)


def _tok_cumsum(m, lincl_bf16, ustrict):
    wc = jnp.dot(lincl_bf16, m.astype(BF16), preferred_element_type=F32)
    coltot = wc[ROUTE_S - 1:ROUTE_S, :]
    colpref = jnp.dot(jnp.broadcast_to(coltot, (V7X_SUBLANES, ROUTE_S)), ustrict, precision=HIGHEST,
                      preferred_element_type=F32)[0:1]
    return wc, coltot, colpref


def _route_kernel(lg_ref, idx_ref, q_ref, g_ref, off_ref, cnt_ref, aff_scr, bits_scr, sel_scr, qt_scr, *,
                  n_valid, cap, pc):
    ne = lg_ref.shape[0]
    s_ = ROUTE_S
    c_pad = idx_ref.shape[2]
    si = lax.broadcasted_iota(I32, (s_, s_), 0)
    ji = lax.broadcasted_iota(I32, (s_, s_), 1)
    valid = ji * s_ + si < n_valid
    lincl = (ji <= si).astype(BF16)
    ustrict = (si < ji).astype(F32)
    ones = jnp.ones((s_, s_), BF16)

    ls = [lg_ref[e] for e in range(ne)]
    mx = functools.reduce(jnp.maximum, ls)
    ex = [jnp.exp(l - mx) for l in ls]
    den = functools.reduce(jnp.add, ex)
    for e in range(ne):
        aff = jnp.where(valid, ex[e] / den, -1.0)
        aff_scr[e] = aff
        bits_scr[e] = pltpu.bitcast(aff, I32)

    def bit_step(i, ts):
        bit = jnp.left_shift(jnp.int32(1), 30 - i)
        out = []
        for e in range(ne):
            cand = ts[e] | bit
            ge = (bits_scr[e] >= cand).astype(F32)
            c1 = jnp.sum(jnp.sum(ge, axis=0, keepdims=True), axis=1, keepdims=True)
            out.append(jnp.where(c1 >= cap, cand, ts[e]))
        return tuple(out)

    ts = lax.fori_loop(0, 31, bit_step, tuple(jnp.zeros((1, 1), I32) for _ in range(ne)))

    cnt = jnp.zeros((s_, s_), F32)
    for e in range(ne):
        bits = bits_scr[e]
        gt = bits > ts[e]
        eq = (bits == ts[e]).astype(F32)
        n_gt = jnp.sum(jnp.sum(gt.astype(F32), axis=0, keepdims=True), axis=1, keepdims=True)
        wc, _, colpref = _tok_cumsum(eq, lincl, ustrict)
        rank_eq = wc + colpref - eq
        sel = jnp.where(gt | ((eq > 0.0) & (rank_eq < cap - n_gt)), 1.0, 0.0)
        sel_scr[e] = sel
        qt_scr[e] = cnt
        cnt = cnt + sel
    wc, _, colpref = _tok_cumsum(cnt, lincl, ustrict)
    offs = wc + colpref - cnt
    off_ref[...] = offs
    cnt_ref[...] = cnt

    def per_expert(e, carry):
        m = sel_scr[e]
        qt = qt_scr[e] + offs
        aff = aff_scr[e]
        wc, coltot, colpref = _tok_cumsum(m, lincl, ustrict)
        wc_b = wc.astype(BF16)
        colcum = jnp.dot((lincl.astype(F32) * coltot).astype(BF16), ones, preferred_element_type=F32)
        colpref8 = jnp.broadcast_to(colpref, (V7X_SUBLANES, s_))
        sub = lax.broadcasted_iota(I32, (s_, pc), 0).astype(F32)
        reps = pc // s_
        colcum_t = jnp.concatenate([colcum] * reps, axis=1) if reps > 1 else colcum
        for c0 in range(0, c_pad, pc):
            p = (lax.broadcasted_iota(I32, (s_, pc), 1) + c0).astype(F32)
            p_row = p[0:1]
            blk = jnp.sum((colcum_t <= p).astype(F32), axis=0, keepdims=True)
            oh_j = (sub == blk).astype(F32)
            colvec = jnp.dot(wc_b, oh_j.astype(BF16), preferred_element_type=F32)
            cp = jnp.dot(colpref8, oh_j, precision=HIGHEST, preferred_element_type=F32)[0:1]
            s_idx = jnp.sum((colvec <= p_row - cp).astype(F32), axis=0, keepdims=True)
            oh_s = (sub == s_idx).astype(F32)
            gv = jnp.sum(oh_s * jnp.dot(aff, oh_j, precision=HIGHEST, preferred_element_type=F32),
                         axis=0, keepdims=True)
            qv = jnp.sum(oh_s * jnp.dot(qt, oh_j, precision=HIGHEST, preferred_element_type=F32),
                         axis=0, keepdims=True)
            live = p_row < cap
            idx_ref[e, :, c0:c0 + pc] = jnp.where(live, blk * s_ + s_idx, 0.0).astype(I32)
            q_ref[e, :, c0:c0 + pc] = jnp.where(live, qv, 0.0).astype(I32)
            g_ref[e, :, c0:c0 + pc] = jnp.where(live, gv, 0.0)
        return carry

    lax.fori_loop(0, ne, per_expert, 0)


def moe_route(logits_t, *, cap):
    ne, n = logits_t.shape
    assert n <= ROUTE_TOKENS and cap <= n
    s_ = ROUTE_S
    c_pad = max(cap, s_)
    pc = min(c_pad, 2 * s_)
    assert c_pad % pc == 0
    lg = jnp.pad(logits_t, ((0, 0), (0, ROUTE_TOKENS - n))).reshape(ne, s_, s_).transpose(0, 2, 1)
    full = lambda shape: pl.BlockSpec(shape, lambda i: (0,) * len(shape))
    slot = jax.ShapeDtypeStruct((ne, 1, c_pad), I32)
    idx, q, g, offs, cnt = pl.pallas_call(
        functools.partial(_route_kernel, n_valid=n, cap=cap, pc=pc),
        grid=(1,),
        in_specs=[full((ne, s_, s_))],
        out_specs=[full((ne, 1, c_pad))] * 3 + [full((s_, s_))] * 2,
        out_shape=[slot, slot, jax.ShapeDtypeStruct((ne, 1, c_pad), F32),
                   jax.ShapeDtypeStruct((s_, s_), F32), jax.ShapeDtypeStruct((s_, s_), F32)],
        scratch_shapes=[pltpu.VMEM((ne, s_, s_), F32), pltpu.VMEM((ne, s_, s_), I32),
                        pltpu.VMEM((ne, s_, s_), F32), pltpu.VMEM((ne, s_, s_), F32)],
        compiler_params=_params("arbitrary"),
        name="moe_route",
    )(lg)
    tok = lambda a: a.T.reshape(-1)[:n]
    return idx[:, 0, :cap], q[:, 0, :cap], g[:, 0, :cap], tok(offs), tok(cnt)


def _ffn_kernel(idx_ref, q_ref, g_ref, h_hbm, wg_ref, wu_ref, wd_ref, z_hbm, gbuf, sbuf, gsem, ssem):
    cb = g_ref.shape[0]
    rows = lambda r: pl.ds(pl.multiple_of(r * TOK_ROWS, TOK_ROWS), TOK_ROWS)

    def gather(p):
        return pltpu.make_async_copy(h_hbm.at[rows(idx_ref[0, 0, p]), :], gbuf.at[rows(p), :], gsem)

    def scatter(p):
        return pltpu.make_async_copy(sbuf.at[rows(p), :], z_hbm.at[rows(q_ref[0, 0, p]), :], ssem)

    def each(fn):
        def body(p, c):
            fn(p)
            return c
        lax.fori_loop(0, cb, body, 0)

    each(lambda p: gather(p).start())
    each(lambda p: gather(p).wait())
    x = jnp.concatenate([gbuf[pl.ds(j, cb, stride=TOK_ROWS), :] for j in range(TOK_ROWS)], axis=1).astype(BF16)
    a = jnp.dot(x, wg_ref[...], preferred_element_type=F32)
    u = jnp.dot(x, wu_ref[...], preferred_element_type=F32)
    y = jnp.dot((_silu(a) * u).astype(BF16), wd_ref[...], preferred_element_type=F32) * g_ref[...]
    for j in range(TOK_ROWS):
        sbuf[pl.ds(j, cb, stride=TOK_ROWS), :] = y[:, j * V7X_LANES:(j + 1) * V7X_LANES]
    each(lambda p: scatter(p).start())
    each(lambda p: scatter(p).wait())


def moe_ffn(h_rows, idx, q, g, w_gate, w_up, w_down, *, cb):
    ne, cap = idx.shape
    d, f = w_gate.shape[1:]
    assert cap % cb == 0
    nb = cap // cb
    smem = pl.BlockSpec((1, 1, cb), lambda e, b: (e * nb + b, 0, 0), memory_space=pltpu.SMEM)
    return pl.pallas_call(
        _ffn_kernel,
        grid=(ne, nb),
        in_specs=[
            smem, smem,
            pl.BlockSpec((cb, 1), lambda e, b: (e * nb + b, 0)),
            pl.BlockSpec(memory_space=pl.ANY),
            pl.BlockSpec((None, d, f), lambda e, b: (e, 0, 0)),
            pl.BlockSpec((None, d, f), lambda e, b: (e, 0, 0)),
            pl.BlockSpec((None, f, d), lambda e, b: (e, 0, 0)),
        ],
        out_specs=pl.BlockSpec(memory_space=pl.ANY),
        out_shape=jax.ShapeDtypeStruct((ne * cap * TOK_ROWS, V7X_LANES), F32),
        scratch_shapes=[pltpu.VMEM((cb * TOK_ROWS, V7X_LANES), F32), pltpu.VMEM((cb * TOK_ROWS, V7X_LANES), F32),
                        pltpu.SemaphoreType.DMA(()), pltpu.SemaphoreType.DMA(())],
        compiler_params=_params("arbitrary", "arbitrary"),
        name="moe_ffn",
    )(idx.reshape(ne * nb, 1, cb), q.reshape(ne * nb, 1, cb), g.reshape(ne * cap, 1), h_rows,
      w_gate.astype(BF16), w_up.astype(BF16), w_down.astype(BF16))


def _combine_kernel(boff_ref, x_ref, gate_ref, off_ref, cnt_ref, z_hbm, o_ref, zbuf, sem, *, ch, p_total):
    i = pl.program_id(0)
    start = boff_ref[i]
    end = boff_ref[i + 1]
    lo = off_ref[...]
    hi = lo + cnt_ref[...]
    o_ref[...] = jnp.zeros(o_ref.shape, F32)

    def chunk(c, carry):
        first = start + c * ch
        row0 = jnp.minimum(first, p_total - ch)
        cp = pltpu.make_async_copy(
            z_hbm.at[pl.ds(pl.multiple_of(row0 * TOK_ROWS, TOK_ROWS), ch * TOK_ROWS), :], zbuf, sem)
        cp.start()
        cp.wait()
        qabs = row0 + lax.broadcasted_iota(I32, (1, ch), 1)
        qf = qabs.astype(F32)
        seg = ((qf >= lo) & (qf < hi) & (qabs >= first)).astype(BF16)
        z = jnp.concatenate([zbuf[pl.ds(j, ch, stride=TOK_ROWS), :] for j in range(TOK_ROWS)], axis=1)
        zh = z.astype(BF16)
        zl = (z - zh.astype(F32)).astype(BF16)
        o_ref[...] += (jnp.dot(seg, zh, preferred_element_type=F32) + jnp.dot(seg, zl, preferred_element_type=F32))
        return carry

    lax.fori_loop(0, (end - start + ch - 1) // ch, chunk, 0)
    o_ref[...] = x_ref[...] + gate_ref[...] * o_ref[...]


def moe_combine(x, gate, z_rows, offs, cnt, *, tb, ch=256):
    n, d = x.shape
    p_total = z_rows.shape[0] // TOK_ROWS
    assert n % tb == 0 and p_total >= ch
    boff = jnp.concatenate([offs[::tb], jnp.full((1,), p_total, F32)]).astype(I32)
    col = pl.BlockSpec((tb, 1), lambda i, s: (i, 0))
    return pl.pallas_call(
        functools.partial(_combine_kernel, ch=ch, p_total=p_total),
        grid_spec=pltpu.PrefetchScalarGridSpec(
            num_scalar_prefetch=1,
            grid=(n // tb,),
            in_specs=[pl.BlockSpec((tb, d), lambda i, s: (i, 0)), pl.BlockSpec((1, d), lambda i, s: (0, 0)),
                      col, col, pl.BlockSpec(memory_space=pl.ANY)],
            out_specs=pl.BlockSpec((tb, d), lambda i, s: (i, 0)),
            scratch_shapes=[pltpu.VMEM((ch * TOK_ROWS, V7X_LANES), F32), pltpu.SemaphoreType.DMA(())],
        ),
        out_shape=jax.ShapeDtypeStruct((n, d), F32),
        compiler_params=_params("arbitrary"),
        name="moe_combine",
    )(boff, x, gate.reshape(1, d), offs.reshape(n, 1), cnt.reshape(n, 1), z_rows)


def moe_layer(x, mod, norm2, w_router, w_gate, w_up, w_down):
    n = x.shape[0]
    ne = w_router.shape[1]
    cap = (CAPACITY_FACTOR * n) // ne
    shift, scale, gate = mod
    h_rows, logits_t = moe_prep(x, norm2, scale, shift, w_router, tb=_row_tile(n, 256))
    idx, q, g, offs, cnt = moe_route(logits_t, cap=cap)
    z_rows = moe_ffn(h_rows, idx, q, g, w_gate, w_up, w_down, cb=min(cap, 256))
    return moe_combine(x, gate, z_rows, offs, cnt, tb=_row_tile(n, 256))


def _final_norm_kernel(x_ref, w_ref, o_ref):
    x = x_ref[...]
    o_ref[...] = x * lax.rsqrt(jnp.mean(x * x, axis=-1, keepdims=True) + EPS) * w_ref[...]


def final_norm(x, w, *, tb):
    n, d = x.shape
    return pl.pallas_call(
        _final_norm_kernel,
        grid=(n // tb,),
        in_specs=[pl.BlockSpec((tb, d), lambda i: (i, 0)), pl.BlockSpec((1, d), lambda i: (0, 0))],
        out_specs=pl.BlockSpec((tb, d), lambda i: (i, 0)),
        out_shape=jax.ShapeDtypeStruct((n, d), F32),
        compiler_params=_params("arbitrary"),
        name="final_norm",
    )(x, w.reshape(1, d))


def kernel(x, c, ctx, c_ctx, ada_w, ada_b, norm1_w, norm2_w, final_norm_w, ssd_w_in, ssd_conv_w, ssd_conv_b,
           ssd_dt_bias, ssd_a_log, ssd_d, ssd_norm_w, ssd_w_out, fnet_w_out, moe_w_router, moe_w_gate, moe_w_up,
           moe_w_down):
    depth = ada_w.shape[0]
    d = x.shape[-1]
    assert x.shape[0] == 1, "one sample per call"
    xs, cs = x[0], ctx[0]
    ada = ada_all(c, c_ctx, ada_w, ada_b)
    for i in range(depth):
        k = i // 2
        is_ssd = i % 2 == 0
        ctx_later = any(j % 2 == 0 for j in range(i + 1, depth))
        sh1, sc1, g1, sh2, sc2, g2 = (ada[i, 0, m * d:(m + 1) * d] for m in range(6))
        csh1, csc1, cg1, csh2, csc2, cg2 = (ada[i, 1, m * d:(m + 1) * d] for m in range(6))
        if is_ssd:
            xs, cs_new = ssd_layer(xs, cs, (sh1, sc1, g1), (csh1, csc1, cg1), norm1_w[i], ssd_w_in[k], ssd_conv_w[k],
                                   ssd_conv_b[k], ssd_dt_bias[k], ssd_a_log[k], ssd_d[k], ssd_norm_w[k],
                                   ssd_w_out[k], ctx_later)
        else:
            xs = fnet_layer(xs, (sh1, sc1, g1), norm1_w[i], fnet_w_out[k])
            cs_new = fnet_layer(cs, (csh1, csc1, cg1), norm1_w[i], fnet_w_out[k]) if ctx_later else None
        xs = moe_layer(xs, (sh2, sc2, g2), norm2_w[i], moe_w_router[i], moe_w_gate[i], moe_w_up[i], moe_w_down[i])
        if ctx_later:
            cs = moe_layer(cs_new, (csh2, csc2, cg2), norm2_w[i], moe_w_router[i], moe_w_gate[i], moe_w_up[i],
                           moe_w_down[i])
    return final_norm(xs, final_norm_w, tb=512)[None]
```

```python
import functools
import math

import jax
import jax.numpy as jnp
from jax import lax
from jax.experimental import pallas as pl
from jax.experimental.pallas import tpu as pltpu

F32 = jnp.float32
BF16 = jnp.bfloat16
I32 = jnp.int32
HIGHEST = lax.Precision.HIGHEST

V7X_LANES = 128
V7X_SUBLANES = 8
V7X_VMEM_BYTES = 64 * 1024 * 1024
VMEM_LIMIT = V7X_VMEM_BYTES - 8 * 1024 * 1024

GRID_W = 64
HEAD_DIM = 64
N_GROUPS = 8
D_STATE = 128
CHUNK = 128
FNET_GROUPS = 8
N_EXPERTS = 16
CAPACITY_FACTOR = 2
EPS = 1e-6
TOK_ROWS = 16


def _params(*sem):
    return pltpu.CompilerParams(dimension_semantics=sem, vmem_limit_bytes=VMEM_LIMIT)


def _rms_mod(x, gain, scale, shift):
    y = x * lax.rsqrt(jnp.mean(x * x, axis=-1, keepdims=True) + EPS)
    return y * gain * (1.0 + scale) + shift


def _silu(x):
    return x * jax.nn.sigmoid(x)


def _ada_kernel(c_ref, w_ref, b_ref, o_ref):
    w = w_ref[...]
    rows = []
    for r in range(2):
        s = _silu(c_ref[r])
        rows.append(jnp.sum(s * w, axis=0, keepdims=True) + b_ref[...])
    rows.append(jnp.zeros((V7X_SUBLANES - 2, w.shape[1]), F32))
    o_ref[...] = jnp.concatenate(rows, axis=0)


def ada_all(c, c_ctx, ada_w, ada_b):
    depth, k, n6 = ada_w.shape
    tn = n6 // 8
    cond = jnp.stack([c[0], c_ctx], axis=0)[:, :, None]
    out = pl.pallas_call(
        _ada_kernel,
        grid=(depth, n6 // tn),
        in_specs=[
            pl.BlockSpec((2, k, 1), lambda l, n: (0, 0, 0)),
            pl.BlockSpec((None, k, tn), lambda l, n: (l, 0, n)),
            pl.BlockSpec((None, 1, tn), lambda l, n: (l, 0, n)),
        ],
        out_specs=pl.BlockSpec((None, V7X_SUBLANES, tn), lambda l, n: (l, 0, n)),
        out_shape=jax.ShapeDtypeStruct((depth, V7X_SUBLANES, n6), F32),
        compiler_params=_params("arbitrary", "arbitrary"),
        name="ada",
    )(cond, ada_w, ada_b[:, None, :])
    return out


def _mm_kernel(*refs, prologue, epilogue):
    it = iter(refs)
    a_ref = next(it)
    if prologue:
        gain_ref, sc_ref, sh_ref = next(it), next(it), next(it)
    w_ref = next(it)
    if epilogue:
        res_ref, gate_ref = next(it), next(it)
    o_ref = next(it)
    a_scr = next(it, None)

    if a_scr is None:
        a_bf16 = a_ref[...]
    else:
        @pl.when(pl.program_id(1) == 0)
        def _():
            a = a_ref[...]
            if prologue:
                a = _rms_mod(a, gain_ref[...], sc_ref[...], sh_ref[...])
            a_scr[...] = a.astype(BF16)

        a_bf16 = a_scr[...]
    acc = jnp.dot(a_bf16, w_ref[...], preferred_element_type=F32)
    if epilogue:
        acc = res_ref[...] + gate_ref[...] * acc
    o_ref[...] = acc.astype(o_ref.dtype)


def matmul(a, w_bf16, *, w_index=None, mod=None, res=None, gate=None, tm, tn, out_dtype=F32, name="mm"):
    m, k = a.shape
    n = w_bf16.shape[-1]
    assert m % tm == 0 and n % tn == 0
    use_scr = mod is not None or a.dtype != BF16
    row = lambda i, j: (0, 0)
    ins, specs = [a], [pl.BlockSpec((tm, k), lambda i, j: (i, 0))]
    if mod is not None:
        for v in mod:
            ins.append(v.reshape(1, k))
            specs.append(pl.BlockSpec((1, k), row))
    ins.append(w_bf16)
    if w_index is None:
        specs.append(pl.BlockSpec((k, tn), lambda i, j: (0, j)))
    else:
        specs.append(pl.BlockSpec((None, k, tn), lambda i, j: (w_index, 0, j)))
    if res is not None:
        ins += [res, gate.reshape(1, n)]
        specs += [pl.BlockSpec((tm, tn), lambda i, j: (i, j)), pl.BlockSpec((1, tn), lambda i, j: (0, j))]
    return pl.pallas_call(
        functools.partial(_mm_kernel, prologue=mod is not None, epilogue=res is not None),
        grid=(m // tm, n // tn),
        in_specs=specs,
        out_specs=pl.BlockSpec((tm, tn), lambda i, j: (i, j)),
        out_shape=jax.ShapeDtypeStruct((m, n), out_dtype),
        scratch_shapes=[pltpu.VMEM((tm, k), BF16)] if use_scr else [],
        compiler_params=_params("arbitrary", "arbitrary"),
        name=name,
    )(*ins)


def _conv_kernel(prev_ref, cur_ref, next_ref, w_ref, b_ref, o_ref, *, width):
    i = pl.program_id(0)
    last = pl.num_programs(0) - 1
    cur = cur_ref[...]
    tb, cb = cur.shape
    prev = jnp.where(i > 0, prev_ref[...], 0.0)
    nxt = jnp.where(i < last, next_ref[...], 0.0)
    if tb > width:
        up = jnp.concatenate([prev, cur[: tb - width]], axis=0)
        dn = jnp.concatenate([cur[width:], nxt], axis=0)
    else:
        up, dn = prev, nxt
    w = w_ref[...]

    def tap(dw):
        return up * w[dw:dw + 1] + cur * w[3 + dw:4 + dw] + dn * w[6 + dw:7 + dw]

    col = lax.broadcasted_iota(I32, (tb, cb), 0) & (width - 1)
    left = jnp.where(col == 0, 0.0, pltpu.roll(tap(0), 1, axis=0))
    right = jnp.where(col == width - 1, 0.0, pltpu.roll(tap(2), tb - 1, axis=0))
    o_ref[...] = _silu(left + tap(1) + right + b_ref[...])


def conv_silu(zx, conv_w, conv_b, *, width, col0, tb, cb=512):
    n = zx.shape[0]
    c = conv_w.shape[-1]
    assert width & (width - 1) == 0 and tb % width == 0 and n % tb == 0 and c % cb == 0 and col0 % cb == 0
    cblk0 = col0 // cb
    per = tb // width
    nrow = n // width
    return pl.pallas_call(
        functools.partial(_conv_kernel, width=width),
        grid=(n // tb, c // cb),
        in_specs=[
            pl.BlockSpec((width, cb), lambda i, j: (jnp.maximum(i * per - 1, 0), cblk0 + j)),
            pl.BlockSpec((tb, cb), lambda i, j: (i, cblk0 + j)),
            pl.BlockSpec((width, cb), lambda i, j: (jnp.minimum((i + 1) * per, nrow - 1), cblk0 + j)),
            pl.BlockSpec((9, cb), lambda i, j: (0, j)),
            pl.BlockSpec((1, cb), lambda i, j: (0, j)),
        ],
        out_specs=pl.BlockSpec((tb, cb), lambda i, j: (i, j)),
        out_shape=jax.ShapeDtypeStruct((n, c), F32),
        compiler_params=_params("arbitrary", "arbitrary"),
        name="conv",
    )(zx, zx, zx, conv_w.reshape(9, c), conv_b.reshape(1, c))


def _dt_kernel(x_ref, b_ref, o_ref):
    v = x_ref[...] + b_ref[...]
    sp = jnp.maximum(v, 0.0) + jnp.log1p(jnp.exp(-jnp.abs(v)))
    o_ref[...] = sp.T


def dt_rows(zx, dt_bias, *, col0, tb):
    n = zx.shape[0]
    w = dt_bias.size
    assert col0 % w == 0 and n % tb == 0
    return pl.pallas_call(
        _dt_kernel,
        grid=(n // tb,),
        in_specs=[pl.BlockSpec((tb, w), lambda i: (i, col0 // w)), pl.BlockSpec((1, w), lambda i: (0, 0))],
        out_specs=pl.BlockSpec((w, tb), lambda i: (0, i)),
        out_shape=jax.ShapeDtypeStruct((w, n), F32),
        compiler_params=_params("arbitrary"),
        name="dt",
    )(zx, dt_bias.reshape(1, w))


def _scan_kernel(x_ref, b_ref, c_ref, dt_ref, a_ref, s0_ref, *rest, rev, has_prev):
    if has_prev:
        yp_ref, y_ref, sf_ref, st, rows = rest
    else:
        y_ref, sf_ref, st, rows = rest
    step = pl.program_id(0)
    q = CHUNK
    nh = dt_ref.shape[0]
    hpg = nh // N_GROUPS
    half = V7X_LANES // 2

    @pl.when(step == 0)
    def _():
        st[...] = s0_ref[...]

    ri = lax.broadcasted_iota(I32, (q, q), 0)
    ci = lax.broadcasted_iota(I32, (q, q), 1)
    if rev:
        upper, keep = ri >= ci, ri <= ci
    else:
        upper, keep = ri <= ci, ri >= ci
    lm = keep.astype(F32)
    dt = dt_ref[...]
    da = dt * -jnp.exp(a_ref[...])
    acs = jnp.dot(da, upper.astype(F32), precision=HIGHEST, preferred_element_type=F32)
    tot = jnp.sum(da, axis=1, keepdims=True)
    rows[0] = dt
    rows[1] = da
    rows[2] = acs
    rows[3] = dt * jnp.exp(tot - acs)
    rows[4] = jnp.broadcast_to(jnp.exp(tot), (nh, q))

    lane_lo = lax.broadcasted_iota(I32, (q, V7X_LANES), 1) < half
    r2 = lax.broadcasted_iota(I32, (2 * D_STATE, V7X_LANES), 0) < D_STATE
    l2 = lax.broadcasted_iota(I32, (2 * D_STATE, V7X_LANES), 1) < half
    diag = r2 == l2

    def group(g, carry):
        r0 = pl.multiple_of(g * hpg, hpg)
        dtg = rows[0, pl.ds(r0, hpg), :]
        dag = rows[1, pl.ds(r0, hpg), :]
        acg = rows[2, pl.ds(r0, hpg), :]
        wtg = rows[3, pl.ds(r0, hpg), :]
        etg = rows[4, pl.ds(r0, hpg), :]
        c0 = pl.multiple_of(g * D_STATE, D_STATE)
        cg = c_ref[:, pl.ds(c0, D_STATE)]
        bg = b_ref[:, pl.ds(c0, D_STATE)]
        cb = lax.dot_general(cg.astype(BF16), bg.astype(BF16), (((1,), (1,)), ((), ())),
                             preferred_element_type=F32)
        bgt = bg.T
        for k in range(hpg // 2):
            x0 = pl.multiple_of(g * (hpg * HEAD_DIM) + k * V7X_LANES, V7X_LANES)
            x2 = x_ref[:, pl.ds(x0, V7X_LANES)]
            ms, ces, bws, ess = [], [], [], []
            for t in range(2):
                j = 2 * k + t
                acol = jnp.sum(lm * dag[j:j + 1, :], axis=1, keepdims=True)
                dec = jnp.where(keep, jnp.exp(acol - acg[j:j + 1, :]), 0.0)
                ms.append(cb * dec * dtg[j:j + 1, :])
                ces.append(cg * jnp.exp(acol))
                bws.append(bgt * wtg[j:j + 1, :])
                ess.append(jnp.broadcast_to(etg[j:j + 1, :], (D_STATE, V7X_LANES)))
            sp = st[g * (hpg // 2) + k]
            lhs = jnp.concatenate(ms + ces, axis=1).astype(BF16)
            rhs = jnp.concatenate([jnp.where(lane_lo, x2, 0.0), jnp.where(lane_lo, 0.0, x2), sp],
                                  axis=0).astype(BF16)
            yp = jnp.dot(lhs, rhs, preferred_element_type=F32)
            if has_prev:
                yp = yp + yp_ref[:, pl.ds(x0, V7X_LANES)]
            y_ref[:, pl.ds(x0, V7X_LANES)] = yp
            upd = jnp.dot(jnp.concatenate(bws, axis=0).astype(BF16), x2.astype(BF16),
                          preferred_element_type=F32)
            st[g * (hpg // 2) + k] = sp * jnp.concatenate(ess, axis=0) + jnp.where(diag, upd, 0.0)
        return carry

    lax.fori_loop(0, N_GROUPS, group, 0)

    @pl.when(step == pl.num_programs(0) - 1)
    def _():
        sf_ref[...] = st[...]


def ssd_scan_dir(xbc, dt_t, a_row, s0, y_prev, *, direction):
    n = xbc.shape[0]
    nh = dt_t.shape[0] // 2
    d_inner = nh * HEAD_DIM
    gn = N_GROUPS * D_STATE
    assert n % CHUNK == 0 and d_inner % gn == 0
    nc = n // CHUNK
    rev = direction == 1
    cidx = (lambda i: nc - 1 - i) if rev else (lambda i: i)
    ins = [xbc, xbc, xbc, dt_t, a_row, s0]
    specs = [
        pl.BlockSpec((CHUNK, d_inner), lambda i: (cidx(i), 0)),
        pl.BlockSpec((CHUNK, gn), lambda i: (cidx(i), d_inner // gn)),
        pl.BlockSpec((CHUNK, gn), lambda i: (cidx(i), d_inner // gn + 1)),
        pl.BlockSpec((nh, CHUNK), lambda i: (direction, cidx(i))),
        pl.BlockSpec((nh, CHUNK), lambda i: (0, 0)),
        pl.BlockSpec(s0.shape, lambda i: (0, 0, 0)),
    ]
    if y_prev is not None:
        ins.append(y_prev)
        specs.append(pl.BlockSpec((CHUNK, d_inner), lambda i: (cidx(i), 0)))
    return pl.pallas_call(
        functools.partial(_scan_kernel, rev=rev, has_prev=y_prev is not None),
        grid=(nc,),
        in_specs=specs,
        out_specs=[pl.BlockSpec((CHUNK, d_inner), lambda i: (cidx(i), 0)),
                   pl.BlockSpec(s0.shape, lambda i: (0, 0, 0))],
        out_shape=[jax.ShapeDtypeStruct((n, d_inner), F32), jax.ShapeDtypeStruct(s0.shape, F32)],
        scratch_shapes=[pltpu.VMEM(s0.shape, F32), pltpu.VMEM((5, nh, CHUNK), F32)],
        compiler_params=_params("arbitrary"),
        name="scan_bwd" if rev else "scan_fwd",
    )(*ins)


def _gate_kernel(y_ref, xs_ref, z_ref, d_ref, nw_ref, o_ref):
    v = (y_ref[...] + xs_ref[...] * d_ref[...]) * _silu(z_ref[...])
    gw = v.shape[1] // N_GROUPS
    for g in range(N_GROUPS):
        vg = v[:, g * gw:(g + 1) * gw]
        r = lax.rsqrt(jnp.mean(vg * vg, axis=-1, keepdims=True) + EPS)
        o_ref[:, g * gw:(g + 1) * gw] = (vg * r * nw_ref[:, g * gw:(g + 1) * gw]).astype(o_ref.dtype)


def ssd_gate(y, xbc, zx, d_row, norm_w, *, tb):
    n, d_inner = y.shape
    blk = pl.BlockSpec((tb, d_inner), lambda i: (i, 0))
    vec = pl.BlockSpec((1, d_inner), lambda i: (0, 0))
    return pl.pallas_call(
        _gate_kernel,
        grid=(n // tb,),
        in_specs=[blk, blk, blk, vec, vec],
        out_specs=blk,
        out_shape=jax.ShapeDtypeStruct((n, d_inner), BF16),
        compiler_params=_params("arbitrary"),
        name="ssd_gate",
    )(y, xbc, zx, d_row.reshape(1, d_inner), norm_w.reshape(1, d_inner))


def _row_tile(n, pref):
    return pref if n % pref == 0 else n


def ssd_layer(x, ctx_s, mod_l, mod_c, norm1, w_in, conv_w, conv_b, dt_bias, a_log, d_skip, norm_w, w_out, k, ctx_out):
    nh = a_log.shape[1]
    d_inner = nh * HEAD_DIM
    gn = N_GROUPS * D_STATE
    a_rows = jnp.broadcast_to(a_log.astype(F32)[:, :, None], (2, nh, CHUNK))
    d_row = jnp.repeat(d_skip.astype(F32), HEAD_DIM)
    in_tn = w_in.shape[-1] // 9

    def project(v, mod, width):
        n = v.shape[0]
        zx = matmul(v, w_in, w_index=k, mod=(norm1, mod[1], mod[0]), tm=_row_tile(n, 1024), tn=in_tn,
                    name="ssd_in")
        xbc = conv_silu(zx, conv_w, conv_b, width=width, col0=d_inner, tb=_row_tile(n, 1024))
        dt_t = dt_rows(zx, dt_bias, col0=2 * d_inner + 2 * gn, tb=_row_tile(n, 512))
        return zx, xbc, dt_t

    zx_c, xbc_c, dt_c = project(ctx_s, mod_c, ctx_s.shape[0])
    zx_l, xbc_l, dt_l = project(x, mod_l, GRID_W)
    zero = jnp.zeros((nh // 2, 2 * D_STATE, V7X_LANES), F32)
    yc, s_f = ssd_scan_dir(xbc_c, dt_c, a_rows[0], zero, None, direction=0)
    yl, _ = ssd_scan_dir(xbc_l, dt_l, a_rows[0], s_f, None, direction=0)
    yc, s_b = ssd_scan_dir(xbc_c, dt_c, a_rows[1], zero, yc, direction=1)
    yl, _ = ssd_scan_dir(xbc_l, dt_l, a_rows[1], s_b, yl, direction=1)

    def output(v, y, xbc, zx, gate):
        n = v.shape[0]
        yn = ssd_gate(y, xbc, zx, d_row, norm_w, tb=_row_tile(n, 256))
        return matmul(yn, w_out, w_index=k, res=v, gate=gate, tm=_row_tile(n, 512), tn=1024, name="ssd_out")

    x_new = output(x, yl, xbc_l, zx_l, mod_l[2])
    ctx_new = output(ctx_s, yc, xbc_c, zx_c, mod_c[2]) if ctx_out else None
    return x_new, ctx_new


def _dft_tables(n):
    j = jnp.arange(n, dtype=I32)
    ang = ((j[:, None] * j[None, :]) % n).astype(F32) * (2.0 * math.pi / n)
    return jnp.cos(ang), jnp.sin(ang)


def _fchan_kernel(x_ref, gain_ref, sc_ref, sh_ref, cs_ref, o_ref):
    h = _rms_mod(x_ref[...], gain_ref[...], sc_ref[...], sh_ref[...]).astype(BF16)
    gw = cs_ref.shape[0]
    d = h.shape[1]
    for g in range(d // gw):
        r = jnp.dot(h[:, g * gw:(g + 1) * gw], cs_ref[...], preferred_element_type=F32)
        o_ref[:, g * gw:(g + 1) * gw] = r[:, :gw]
        o_ref[:, d + g * gw:d + (g + 1) * gw] = r[:, gw:]


def _f1_kernel(r_ref, i_ref, k1_ref, tc_ref, ts_ref, yr_ref, yi_ref):
    la, sub, cw = r_ref.shape
    rows = la * sub
    xx = jnp.concatenate([r_ref[...].reshape(rows, cw), i_ref[...].reshape(rows, cw)], axis=0).astype(BF16)
    y = jnp.dot(k1_ref[...], xx, preferred_element_type=F32)
    yr, yi = y[:rows], y[rows:]
    c, s = tc_ref[...], ts_ref[...]
    yr_ref[...] = (c * yr + s * yi).reshape(la, sub, cw)
    yi_ref[...] = (c * yi - s * yr).reshape(la, sub, cw)


def _f2_kernel(r_ref, i_ref, k2_ref, o_ref):
    sub, lb, cw = r_ref.shape
    xx = jnp.concatenate([r_ref[...].reshape(sub * lb, cw), i_ref[...].reshape(sub * lb, cw)],
                         axis=0).astype(BF16)
    o_ref[...] = jnp.dot(k2_ref[...], xx, preferred_element_type=F32).reshape(lb, sub, cw)


def fnet_layer(x, mod, norm1, w_out, k, *, cw=512):
    l, d = x.shape
    sub = V7X_SUBLANES
    la = l // 128 if l >= 1024 else sub
    lb = l // la
    gw = d // FNET_GROUPS
    assert la % sub == 0 and lb % sub == 0 and d % cw == 0
    shift, scale, gate = mod
    cc, sc = _dft_tables(gw)
    norm = 1.0 / math.sqrt(l * gw)
    cs = (jnp.concatenate([cc, -sc], axis=1) * norm).astype(BF16)
    tm = _row_tile(l, 512)
    vec = pl.BlockSpec((1, d), lambda i: (0, 0))
    g2 = pl.pallas_call(
        _fchan_kernel,
        grid=(l // tm,),
        in_specs=[pl.BlockSpec((tm, d), lambda i: (i, 0)), vec, vec, vec, pl.BlockSpec((gw, 2 * gw), lambda i: (0, 0))],
        out_specs=pl.BlockSpec((tm, 2 * d), lambda i: (i, 0)),
        out_shape=jax.ShapeDtypeStruct((l, 2 * d), F32),
        compiler_params=_params("arbitrary"),
        name="fnet_chan",
    )(x, norm1.reshape(1, d), scale.reshape(1, d), shift.reshape(1, d), cs)
    ca, sa = _dft_tables(la)
    eye = jnp.eye(sub, dtype=F32)
    k1 = jnp.concatenate([jnp.concatenate([jnp.kron(ca, eye), jnp.kron(sa, eye)], axis=1),
                          jnp.concatenate([jnp.kron(-sa, eye), jnp.kron(ca, eye)], axis=1)], axis=0).astype(BF16)
    bi = jnp.arange(lb, dtype=I32).reshape(lb // sub, 1, sub)
    di = jnp.arange(la, dtype=I32).reshape(1, la, 1)
    ang = ((bi * di) % l).astype(F32) * (2.0 * math.pi / l)
    tc = jnp.cos(ang).reshape(lb // sub, la * sub, 1)
    ts = jnp.sin(ang).reshape(lb // sub, la * sub, 1)
    g3 = g2.reshape(la, lb, 2 * d)
    ncb = d // cw
    blk1 = lambda off: pl.BlockSpec((la, sub, cw), lambda b, c: (0, b, off + c))
    twb = pl.BlockSpec((None, la * sub, 1), lambda b, c: (b, 0, 0))
    yr3, yi3 = pl.pallas_call(
        _f1_kernel,
        grid=(lb // sub, ncb),
        in_specs=[blk1(0), blk1(ncb), pl.BlockSpec(k1.shape, lambda b, c: (0, 0)), twb, twb],
        out_specs=[blk1(0), blk1(0)],
        out_shape=[jax.ShapeDtypeStruct((la, lb, d), F32)] * 2,
        compiler_params=_params("arbitrary", "arbitrary"),
        name="fnet_stage1",
    )(g3, g3, k1, tc, ts)
    cb_, sb_ = _dft_tables(lb)
    k2 = jnp.concatenate([jnp.einsum("cb,de->cdeb", cb_, eye).reshape(lb * sub, sub * lb),
                          jnp.einsum("cb,de->cdeb", sb_, eye).reshape(lb * sub, sub * lb)], axis=1).astype(BF16)
    blk2 = pl.BlockSpec((sub, lb, cw), lambda dd, c: (dd, 0, c))
    f3 = pl.pallas_call(
        _f2_kernel,
        grid=(la // sub, ncb),
        in_specs=[blk2, blk2, pl.BlockSpec(k2.shape, lambda dd, c: (0, 0))],
        out_specs=pl.BlockSpec((lb, sub, cw), lambda dd, c: (0, dd, c)),
        out_shape=jax.ShapeDtypeStruct((lb, la, d), F32),
        compiler_params=_params("arbitrary", "arbitrary"),
        name="fnet_stage2",
    )(yr3, yi3, k2)
    return matmul(f3.reshape(l, d), w_out, w_index=k, res=x, gate=gate, tm=_row_tile(l, 512), tn=1024,
                  name="fnet_out")


ROUTE_S = 128
ROUTE_TOKENS = ROUTE_S * ROUTE_S


def _moe_prep_kernel(x_ref, gain_ref, sc_ref, sh_ref, wr_ref, h_ref, lg_ref):
    h = _rms_mod(x_ref[...], gain_ref[...], sc_ref[...], sh_ref[...])
    tb = h.shape[0]
    for j in range(TOK_ROWS):
        h_ref[pl.ds(j, tb, stride=TOK_ROWS), :] = h[:, j * V7X_LANES:(j + 1) * V7X_LANES]
    lg_ref[...] = lax.dot_general(wr_ref[...], h, (((1,), (1,)), ((), ())), precision=HIGHEST,
                                  preferred_element_type=F32)


def moe_prep(x, norm2, scale, shift, w_router, *, tb):
    n, d = x.shape
    e = w_router.shape[1]
    assert d == TOK_ROWS * V7X_LANES and n % tb == 0
    vec = pl.BlockSpec((1, d), lambda i: (0, 0))
    return pl.pallas_call(
        _moe_prep_kernel,
        grid=(n // tb,),
        in_specs=[pl.BlockSpec((tb, d), lambda i: (i, 0)), vec, vec, vec, pl.BlockSpec((e, d), lambda i: (0, 0))],
        out_specs=[pl.BlockSpec((tb * TOK_ROWS, V7X_LANES), lambda i: (i, 0)), pl.BlockSpec((e, tb), lambda i: (0, i))],
        out_shape=[jax.ShapeDtypeStruct((n * TOK_ROWS, V7X_LANES), F32), jax.ShapeDtypeStruct((e, n), F32)],
        compiler_params=_params("arbitrary"),
        name="moe_prep",
    )(x, norm2.reshape(1, d), scale.reshape(1, d), shift.reshape(1, d), w_router.T)


def _tok_cumsum(m, lincl_bf16, ustrict):
    wc = jnp.dot(lincl_bf16, m.astype(BF16), preferred_element_type=F32)
    coltot = wc[ROUTE_S - 1:ROUTE_S, :]
    colpref = jnp.dot(jnp.broadcast_to(coltot, (V7X_SUBLANES, ROUTE_S)), ustrict, precision=HIGHEST,
                      preferred_element_type=F32)[0:1]
    return wc, coltot, colpref


def _route_kernel(lg_ref, idx_ref, q_ref, g_ref, off_ref, cnt_ref, aff_scr, bits_scr, sel_scr, qt_scr, *,
                  n_valid, cap, pc):
    ne = lg_ref.shape[0]
    s_ = ROUTE_S
    c_pad = idx_ref.shape[2]
    si = lax.broadcasted_iota(I32, (s_, s_), 0)
    ji = lax.broadcasted_iota(I32, (s_, s_), 1)
    valid = ji * s_ + si < n_valid
    lincl = (ji <= si).astype(BF16)
    ustrict = (si < ji).astype(F32)
    ones = jnp.ones((s_, s_), BF16)

    ls = [lg_ref[e] for e in range(ne)]
    mx = functools.reduce(jnp.maximum, ls)
    ex = [jnp.exp(l - mx) for l in ls]
    den = functools.reduce(jnp.add, ex)
    for e in range(ne):
        aff = jnp.where(valid, ex[e] / den, -1.0)
        aff_scr[e] = aff
        bits_scr[e] = pltpu.bitcast(aff, I32)

    def bit_step(i, ts):
        bit = jnp.left_shift(jnp.int32(1), 30 - i)
        out = []
        for e in range(ne):
            cand = ts[e] | bit
            ge = (bits_scr[e] >= cand).astype(F32)
            c1 = jnp.sum(jnp.sum(ge, axis=0, keepdims=True), axis=1, keepdims=True)
            out.append(jnp.where(c1 >= cap, cand, ts[e]))
        return tuple(out)

    ts = lax.fori_loop(0, 31, bit_step, tuple(jnp.zeros((1, 1), I32) for _ in range(ne)))

    cnt = jnp.zeros((s_, s_), F32)
    for e in range(ne):
        bits = bits_scr[e]
        gt = bits > ts[e]
        eq = (bits == ts[e]).astype(F32)
        n_gt = jnp.sum(jnp.sum(gt.astype(F32), axis=0, keepdims=True), axis=1, keepdims=True)
        wc, _, colpref = _tok_cumsum(eq, lincl, ustrict)
        rank_eq = wc + colpref - eq
        sel = jnp.where(gt | ((eq > 0.0) & (rank_eq < cap - n_gt)), 1.0, 0.0)
        sel_scr[e] = sel
        qt_scr[e] = cnt
        cnt = cnt + sel
    wc, _, colpref = _tok_cumsum(cnt, lincl, ustrict)
    offs = wc + colpref - cnt
    off_ref[...] = offs
    cnt_ref[...] = cnt

    def per_expert(e, carry):
        m = sel_scr[e]
        qt = qt_scr[e] + offs
        aff = aff_scr[e]
        wc, coltot, colpref = _tok_cumsum(m, lincl, ustrict)
        wc_b = wc.astype(BF16)
        colcum = jnp.dot((lincl.astype(F32) * coltot).astype(BF16), ones, preferred_element_type=F32)
        colpref8 = jnp.broadcast_to(colpref, (V7X_SUBLANES, s_))
        sub = lax.broadcasted_iota(I32, (s_, pc), 0).astype(F32)
        reps = pc // s_
        colcum_t = jnp.concatenate([colcum] * reps, axis=1) if reps > 1 else colcum
        for c0 in range(0, c_pad, pc):
            p = (lax.broadcasted_iota(I32, (s_, pc), 1) + c0).astype(F32)
            p_row = p[0:1]
            blk = jnp.sum((colcum_t <= p).astype(F32), axis=0, keepdims=True)
            oh_j = (sub == blk).astype(F32)
            colvec = jnp.dot(wc_b, oh_j.astype(BF16), preferred_element_type=F32)
            cp = jnp.dot(colpref8, oh_j, precision=HIGHEST, preferred_element_type=F32)[0:1]
            s_idx = jnp.sum((colvec <= p_row - cp).astype(F32), axis=0, keepdims=True)
            oh_s = (sub == s_idx).astype(F32)
            gv = jnp.sum(oh_s * jnp.dot(aff, oh_j, precision=HIGHEST, preferred_element_type=F32),
                         axis=0, keepdims=True)
            qv = jnp.sum(oh_s * jnp.dot(qt, oh_j, precision=HIGHEST, preferred_element_type=F32),
                         axis=0, keepdims=True)
            live = p_row < cap
            idx_ref[e, :, c0:c0 + pc] = jnp.where(live, blk * s_ + s_idx, 0.0).astype(I32)
            q_ref[e, :, c0:c0 + pc] = jnp.where(live, qv, 0.0).astype(I32)
            g_ref[e, :, c0:c0 + pc] = jnp.where(live, gv, 0.0)
        return carry

    lax.fori_loop(0, ne, per_expert, 0)


def moe_route(logits_t, *, cap):
    ne, n = logits_t.shape
    assert n <= ROUTE_TOKENS and cap <= n
    s_ = ROUTE_S
    c_pad = max(cap, s_)
    pc = min(c_pad, 2 * s_)
    assert c_pad % pc == 0
    lg = jnp.pad(logits_t, ((0, 0), (0, ROUTE_TOKENS - n))).reshape(ne, s_, s_).transpose(0, 2, 1)
    full = lambda shape: pl.BlockSpec(shape, lambda i: (0,) * len(shape))
    slot = jax.ShapeDtypeStruct((ne, 1, c_pad), I32)
    idx, q, g, offs, cnt = pl.pallas_call(
        functools.partial(_route_kernel, n_valid=n, cap=cap, pc=pc),
        grid=(1,),
        in_specs=[full((ne, s_, s_))],
        out_specs=[full((ne, 1, c_pad))] * 3 + [full((s_, s_))] * 2,
        out_shape=[slot, slot, jax.ShapeDtypeStruct((ne, 1, c_pad), F32),
                   jax.ShapeDtypeStruct((s_, s_), F32), jax.ShapeDtypeStruct((s_, s_), F32)],
        scratch_shapes=[pltpu.VMEM((ne, s_, s_), F32), pltpu.VMEM((ne, s_, s_), I32),
                        pltpu.VMEM((ne, s_, s_), F32), pltpu.VMEM((ne, s_, s_), F32)],
        compiler_params=_params("arbitrary"),
        name="moe_route",
    )(lg)
    tok = lambda a: a.T.reshape(-1)[:n]
    return idx[:, 0, :cap], q[:, 0, :cap], g[:, 0, :cap], tok(offs), tok(cnt)


def _ffn_kernel(idx_ref, q_ref, g_ref, h_hbm, wg_ref, wu_ref, wd_ref, z_hbm, ga, gb, sa, sb, sems):
    step = pl.program_id(0)
    last = pl.num_programs(0) - 1
    cbh = ga.shape[0] // TOK_ROWS
    total = idx_ref.shape[0]
    base = step * (2 * cbh)
    srows = lambda p: pl.ds(p * TOK_ROWS, TOK_ROWS)
    drows = lambda r: pl.ds(pl.multiple_of(r * TOK_ROWS, TOK_ROWS), TOK_ROWS)
    G_A, G_B, S_A, S_B = range(4)

    def gather_start(buf, sem, slot0):
        for p in range(cbh):
            pltpu.make_async_copy(h_hbm.at[drows(idx_ref[slot0 + p]), :], buf.at[srows(p), :], sems.at[sem]).start()

    def gather_wait(buf, sem):
        for p in range(cbh):
            pltpu.make_async_copy(h_hbm.at[srows(0), :], buf.at[srows(p), :], sems.at[sem]).wait()

    def scatter_start(buf, sem, slot0):
        for p in range(cbh):
            pltpu.make_async_copy(buf.at[srows(p), :], z_hbm.at[drows(q_ref[slot0 + p]), :], sems.at[sem]).start()

    def scatter_wait(buf, sem):
        for p in range(cbh):
            pltpu.make_async_copy(buf.at[srows(p), :], z_hbm.at[srows(0), :], sems.at[sem]).wait()

    def compute(gbuf, sbuf, g):
        x = jnp.concatenate([gbuf[pl.ds(j, cbh, stride=TOK_ROWS), :] for j in range(TOK_ROWS)],
                            axis=1).astype(BF16)
        a = jnp.dot(x, wg_ref[...], preferred_element_type=F32)
        u = jnp.dot(x, wu_ref[...], preferred_element_type=F32)
        y = jnp.dot((_silu(a) * u).astype(BF16), wd_ref[...], preferred_element_type=F32) * g
        for j in range(TOK_ROWS):
            sbuf[pl.ds(j, cbh, stride=TOK_ROWS), :] = y[:, j * V7X_LANES:(j + 1) * V7X_LANES]

    @pl.when(step == 0)
    def _():
        gather_start(ga, G_A, 0)

    gather_wait(ga, G_A)

    @pl.when(step > 0)
    def _():
        scatter_wait(sa, S_A)

    gather_start(gb, G_B, base + cbh)
    compute(ga, sa, g_ref[0:cbh, :])
    scatter_start(sa, S_A, base)
    gather_wait(gb, G_B)

    @pl.when(step > 0)
    def _():
        scatter_wait(sb, S_B)

    gather_start(ga, G_A, jnp.minimum(base + 2 * cbh, total - cbh))
    compute(gb, sb, g_ref[cbh:2 * cbh, :])
    scatter_start(sb, S_B, base + cbh)

    @pl.when(step == last)
    def _():
        gather_wait(ga, G_A)
        scatter_wait(sa, S_A)
        scatter_wait(sb, S_B)


def moe_ffn(h_rows, idx, q, g, w_gate, w_up, w_down, layer, *, cbh):
    ne, cap = idx.shape
    d, f = w_gate.shape[2:]
    sb_rows = 2 * cbh
    assert cap % sb_rows == 0
    spe = cap // sb_rows
    wmap = lambda s, *_: (layer, s // spe, 0, 0)
    buf = pltpu.VMEM((cbh * TOK_ROWS, V7X_LANES), F32)
    return pl.pallas_call(
        _ffn_kernel,
        grid_spec=pltpu.PrefetchScalarGridSpec(
            num_scalar_prefetch=2,
            grid=(ne * spe,),
            in_specs=[
                pl.BlockSpec((sb_rows, 1), lambda s, *_: (s, 0)),
                pl.BlockSpec(memory_space=pl.ANY),
                pl.BlockSpec((None, None, d, f), wmap),
                pl.BlockSpec((None, None, d, f), wmap),
                pl.BlockSpec((None, None, f, d), wmap),
            ],
            out_specs=pl.BlockSpec(memory_space=pl.ANY),
            scratch_shapes=[buf, buf, buf, buf, pltpu.SemaphoreType.DMA((4,))],
        ),
        out_shape=jax.ShapeDtypeStruct((ne * cap * TOK_ROWS, V7X_LANES), F32),
        compiler_params=_params("arbitrary"),
        name="moe_ffn",
    )(idx.reshape(-1), q.reshape(-1), g.reshape(ne * cap, 1), h_rows, w_gate, w_up, w_down)


def _combine_kernel(boff_ref, x_ref, gate_ref, off_ref, cnt_ref, z_hbm, o_ref, z0, z1, sems, *, ch, p_total):
    i = pl.program_id(0)
    start = boff_ref[i]
    end = boff_ref[i + 1]
    nch = (end - start + ch - 1) // ch
    lo = off_ref[...]
    hi = lo + cnt_ref[...]
    o_ref[...] = jnp.zeros(o_ref.shape, F32)
    zbufs = (z0, z1)

    def row_start(c):
        return jnp.minimum(start + c * ch, p_total - ch)

    def copy(c, slot):
        src = z_hbm.at[pl.ds(pl.multiple_of(row_start(c) * TOK_ROWS, TOK_ROWS), ch * TOK_ROWS), :]
        return pltpu.make_async_copy(src, zbufs[slot], sems.at[slot])

    def process(c, slot):
        copy(c, slot).wait()

        @pl.when(c + 1 < nch)
        def _():
            copy(c + 1, 1 - slot).start()

        first = start + c * ch
        qabs = row_start(c) + lax.broadcasted_iota(I32, (1, ch), 1)
        qf = qabs.astype(F32)
        seg = ((qf >= lo) & (qf < hi) & (qabs >= first)).astype(BF16)
        zb = zbufs[slot]
        z = jnp.concatenate([zb[pl.ds(j, ch, stride=TOK_ROWS), :] for j in range(TOK_ROWS)], axis=1)
        zh = z.astype(BF16)
        zl = (z - zh.astype(F32)).astype(BF16)
        o_ref[...] += (jnp.dot(seg, zh, preferred_element_type=F32) + jnp.dot(seg, zl, preferred_element_type=F32))

    @pl.when(nch > 0)
    def _():
        copy(0, 0).start()

    def pair(k, carry):
        process(2 * k, 0)

        @pl.when(2 * k + 1 < nch)
        def _():
            process(2 * k + 1, 1)

        return carry

    lax.fori_loop(0, (nch + 1) // 2, pair, 0)
    o_ref[...] = x_ref[...] + gate_ref[...] * o_ref[...]


def moe_combine(x, gate, z_rows, offs, cnt, *, tb, ch=256):
    n, d = x.shape
    p_total = z_rows.shape[0] // TOK_ROWS
    assert n % tb == 0 and p_total >= ch
    boff = jnp.concatenate([offs[::tb], jnp.full((1,), p_total, F32)]).astype(I32)
    col = pl.BlockSpec((tb, 1), lambda i, s: (i, 0))
    return pl.pallas_call(
        functools.partial(_combine_kernel, ch=ch, p_total=p_total),
        grid_spec=pltpu.PrefetchScalarGridSpec(
            num_scalar_prefetch=1,
            grid=(n // tb,),
            in_specs=[pl.BlockSpec((tb, d), lambda i, s: (i, 0)), pl.BlockSpec((1, d), lambda i, s: (0, 0)),
                      col, col, pl.BlockSpec(memory_space=pl.ANY)],
            out_specs=pl.BlockSpec((tb, d), lambda i, s: (i, 0)),
            scratch_shapes=[pltpu.VMEM((ch * TOK_ROWS, V7X_LANES), F32), pltpu.VMEM((ch * TOK_ROWS, V7X_LANES), F32),
                            pltpu.SemaphoreType.DMA((2,))],
        ),
        out_shape=jax.ShapeDtypeStruct((n, d), F32),
        compiler_params=_params("arbitrary"),
        name="moe_combine",
    )(boff, x, gate.reshape(1, d), offs.reshape(n, 1), cnt.reshape(n, 1), z_rows)


def moe_layer(x, mod, norm2, w_router, w_gate, w_up, w_down, layer):
    n = x.shape[0]
    ne = w_router.shape[1]
    cap = (CAPACITY_FACTOR * n) // ne
    shift, scale, gate = mod
    h_rows, logits_t = moe_prep(x, norm2, scale, shift, w_router, tb=_row_tile(n, 256))
    idx, q, g, offs, cnt = moe_route(logits_t, cap=cap)
    z_rows = moe_ffn(h_rows, idx, q, g, w_gate, w_up, w_down, layer, cbh=min(cap // 2, 256))
    return moe_combine(x, gate, z_rows, offs, cnt, tb=_row_tile(n, 512))


def _final_norm_kernel(x_ref, w_ref, o_ref):
    x = x_ref[...]
    o_ref[...] = x * lax.rsqrt(jnp.mean(x * x, axis=-1, keepdims=True) + EPS) * w_ref[...]


def final_norm(x, w, *, tb):
    n, d = x.shape
    return pl.pallas_call(
        _final_norm_kernel,
        grid=(n // tb,),
        in_specs=[pl.BlockSpec((tb, d), lambda i: (i, 0)), pl.BlockSpec((1, d), lambda i: (0, 0))],
        out_specs=pl.BlockSpec((tb, d), lambda i: (i, 0)),
        out_shape=jax.ShapeDtypeStruct((n, d), F32),
        compiler_params=_params("arbitrary"),
        name="final_norm",
    )(x, w.reshape(1, d))


def kernel(x, c, ctx, c_ctx, ada_w, ada_b, norm1_w, norm2_w, final_norm_w, ssd_w_in, ssd_conv_w, ssd_conv_b,
           ssd_dt_bias, ssd_a_log, ssd_d, ssd_norm_w, ssd_w_out, fnet_w_out, moe_w_router, moe_w_gate, moe_w_up,
           moe_w_down):
    depth = ada_w.shape[0]
    d = x.shape[-1]
    assert x.shape[0] == 1, "one sample per call"
    xs, cs = x[0], ctx[0]
    ada = ada_all(c, c_ctx, ada_w, ada_b)
    moe_w = (moe_w_gate.astype(BF16), moe_w_up.astype(BF16), moe_w_down.astype(BF16))
    w_in_b, w_out_b, w_fnet_b = ssd_w_in.astype(BF16), ssd_w_out.astype(BF16), fnet_w_out.astype(BF16)
    for i in range(depth):
        k = i // 2
        is_ssd = i % 2 == 0
        ctx_later = any(j % 2 == 0 for j in range(i + 1, depth))
        sh1, sc1, g1, sh2, sc2, g2 = (ada[i, 0, m * d:(m + 1) * d] for m in range(6))
        csh1, csc1, cg1, csh2, csc2, cg2 = (ada[i, 1, m * d:(m + 1) * d] for m in range(6))
        if is_ssd:
            xs, cs_new = ssd_layer(xs, cs, (sh1, sc1, g1), (csh1, csc1, cg1), norm1_w[i], w_in_b, ssd_conv_w[k],
                                   ssd_conv_b[k], ssd_dt_bias[k], ssd_a_log[k], ssd_d[k], ssd_norm_w[k],
                                   w_out_b, k, ctx_later)
        else:
            xs = fnet_layer(xs, (sh1, sc1, g1), norm1_w[i], w_fnet_b, k)
            cs_new = fnet_layer(cs, (csh1, csc1, cg1), norm1_w[i], w_fnet_b, k) if ctx_later else None
        xs = moe_layer(xs, (sh2, sc2, g2), norm2_w[i], moe_w_router[i], *moe_w, i)
        if ctx_later:
            cs = moe_layer(cs_new, (csh2, csc2, cg2), norm2_w[i], moe_w_router[i], *moe_w, i)
    return final_norm(xs, final_norm_w, tb=512)[None]
```

```python
import functools
import math

import jax
import jax.numpy as jnp
from jax import lax
from jax.experimental import pallas as pl
from jax.experimental.pallas import tpu as pltpu

F32 = jnp.float32
BF16 = jnp.bfloat16
I32 = jnp.int32
HIGHEST = lax.Precision.HIGHEST

V7X_LANES = 128
V7X_SUBLANES = 8
V7X_VMEM_BYTES = 64 * 1024 * 1024
VMEM_LIMIT = V7X_VMEM_BYTES - 8 * 1024 * 1024

GRID_W = 64
HEAD_DIM = 64
N_GROUPS = 8
D_STATE = 128
CHUNK = 128
FNET_GROUPS = 8
N_EXPERTS = 16
CAPACITY_FACTOR = 2
EPS = 1e-6
TOK_ROWS = 16


def _params(*sem):
    return pltpu.CompilerParams(dimension_semantics=sem, vmem_limit_bytes=VMEM_LIMIT)


def _rms_mod(x, gain, scale, shift):
    y = x * lax.rsqrt(jnp.mean(x * x, axis=-1, keepdims=True) + EPS)
    return y * gain * (1.0 + scale) + shift


def _silu(x):
    return x * jax.nn.sigmoid(x)


def _ada_kernel(c_ref, w_ref, b_ref, o_ref):
    w = w_ref[...]
    rows = []
    for r in range(2):
        s = _silu(c_ref[r])
        rows.append(jnp.sum(s * w, axis=0, keepdims=True) + b_ref[...])
    rows.append(jnp.zeros((V7X_SUBLANES - 2, w.shape[1]), F32))
    o_ref[...] = jnp.concatenate(rows, axis=0)


def ada_all(c, c_ctx, ada_w, ada_b):
    depth, k, n6 = ada_w.shape
    tn = n6 // 8
    cond = jnp.stack([c[0], c_ctx], axis=0)[:, :, None]
    out = pl.pallas_call(
        _ada_kernel,
        grid=(depth, n6 // tn),
        in_specs=[
            pl.BlockSpec((2, k, 1), lambda l, n: (0, 0, 0)),
            pl.BlockSpec((None, k, tn), lambda l, n: (l, 0, n)),
            pl.BlockSpec((None, 1, tn), lambda l, n: (l, 0, n)),
        ],
        out_specs=pl.BlockSpec((None, V7X_SUBLANES, tn), lambda l, n: (l, 0, n)),
        out_shape=jax.ShapeDtypeStruct((depth, V7X_SUBLANES, n6), F32),
        compiler_params=_params("arbitrary", "arbitrary"),
        name="ada",
    )(cond, ada_w, ada_b[:, None, :])
    return out


def _mm_kernel(*refs, prologue, epilogue):
    it = iter(refs)
    a_ref = next(it)
    if prologue:
        gain_ref, sc_ref, sh_ref = next(it), next(it), next(it)
    w_ref = next(it)
    if epilogue:
        res_ref, gate_ref = next(it), next(it)
    o_ref = next(it)
    a_scr = next(it, None)

    if a_scr is None:
        a_bf16 = a_ref[...]
    else:
        @pl.when(pl.program_id(1) == 0)
        def _():
            a = a_ref[...]
            if prologue:
                a = _rms_mod(a, gain_ref[...], sc_ref[...], sh_ref[...])
            a_scr[...] = a.astype(BF16)

        a_bf16 = a_scr[...]
    acc = jnp.dot(a_bf16, w_ref[...], preferred_element_type=F32)
    if epilogue:
        acc = res_ref[...] + gate_ref[...] * acc
    o_ref[...] = acc.astype(o_ref.dtype)


def matmul(a, w_bf16, *, w_index=None, mod=None, res=None, gate=None, tm, tn, out_dtype=F32, name="mm"):
    m, k = a.shape
    n = w_bf16.shape[-1]
    assert m % tm == 0 and n % tn == 0
    use_scr = mod is not None or a.dtype != BF16
    row = lambda i, j: (0, 0)
    ins, specs = [a], [pl.BlockSpec((tm, k), lambda i, j: (i, 0))]
    if mod is not None:
        for v in mod:
            ins.append(v.reshape(1, k))
            specs.append(pl.BlockSpec((1, k), row))
    ins.append(w_bf16)
    if w_index is None:
        specs.append(pl.BlockSpec((k, tn), lambda i, j: (0, j)))
    else:
        specs.append(pl.BlockSpec((None, k, tn), lambda i, j: (w_index, 0, j)))
    if res is not None:
        ins += [res, gate.reshape(1, n)]
        specs += [pl.BlockSpec((tm, tn), lambda i, j: (i, j)), pl.BlockSpec((1, tn), lambda i, j: (0, j))]
    return pl.pallas_call(
        functools.partial(_mm_kernel, prologue=mod is not None, epilogue=res is not None),
        grid=(m // tm, n // tn),
        in_specs=specs,
        out_specs=pl.BlockSpec((tm, tn), lambda i, j: (i, j)),
        out_shape=jax.ShapeDtypeStruct((m, n), out_dtype),
        scratch_shapes=[pltpu.VMEM((tm, k), BF16)] if use_scr else [],
        compiler_params=_params("arbitrary", "arbitrary"),
        name=name,
    )(*ins)


def _conv_kernel(prev_ref, cur_ref, next_ref, w_ref, b_ref, o_ref, *, width):
    i = pl.program_id(0)
    last = pl.num_programs(0) - 1
    cur = cur_ref[...]
    tb, cb = cur.shape
    prev = jnp.where(i > 0, prev_ref[...], 0.0)
    nxt = jnp.where(i < last, next_ref[...], 0.0)
    if tb > width:
        up = jnp.concatenate([prev, cur[: tb - width]], axis=0)
        dn = jnp.concatenate([cur[width:], nxt], axis=0)
    else:
        up, dn = prev, nxt
    w = w_ref[...]

    def tap(dw):
        return up * w[dw:dw + 1] + cur * w[3 + dw:4 + dw] + dn * w[6 + dw:7 + dw]

    col = lax.broadcasted_iota(I32, (tb, cb), 0) & (width - 1)
    left = jnp.where(col == 0, 0.0, pltpu.roll(tap(0), 1, axis=0))
    right = jnp.where(col == width - 1, 0.0, pltpu.roll(tap(2), tb - 1, axis=0))
    o_ref[...] = _silu(left + tap(1) + right + b_ref[...])


def conv_silu(zx, conv_w, conv_b, *, width, col0, tb, cb=512):
    n = zx.shape[0]
    c = conv_w.shape[-1]
    assert width & (width - 1) == 0 and tb % width == 0 and n % tb == 0 and c % cb == 0 and col0 % cb == 0
    cblk0 = col0 // cb
    per = tb // width
    nrow = n // width
    return pl.pallas_call(
        functools.partial(_conv_kernel, width=width),
        grid=(n // tb, c // cb),
        in_specs=[
            pl.BlockSpec((width, cb), lambda i, j: (jnp.maximum(i * per - 1, 0), cblk0 + j)),
            pl.BlockSpec((tb, cb), lambda i, j: (i, cblk0 + j)),
            pl.BlockSpec((width, cb), lambda i, j: (jnp.minimum((i + 1) * per, nrow - 1), cblk0 + j)),
            pl.BlockSpec((9, cb), lambda i, j: (0, j)),
            pl.BlockSpec((1, cb), lambda i, j: (0, j)),
        ],
        out_specs=pl.BlockSpec((tb, cb), lambda i, j: (i, j)),
        out_shape=jax.ShapeDtypeStruct((n, c), F32),
        compiler_params=_params("arbitrary", "arbitrary"),
        name="conv",
    )(zx, zx, zx, conv_w.reshape(9, c), conv_b.reshape(1, c))


def _dt_kernel(x_ref, b_ref, o_ref):
    v = x_ref[...] + b_ref[...]
    sp = jnp.maximum(v, 0.0) + jnp.log1p(jnp.exp(-jnp.abs(v)))
    o_ref[...] = sp.T


def dt_rows(zx, dt_bias, *, col0, tb):
    n = zx.shape[0]
    w = dt_bias.size
    assert col0 % w == 0 and n % tb == 0
    return pl.pallas_call(
        _dt_kernel,
        grid=(n // tb,),
        in_specs=[pl.BlockSpec((tb, w), lambda i: (i, col0 // w)), pl.BlockSpec((1, w), lambda i: (0, 0))],
        out_specs=pl.BlockSpec((w, tb), lambda i: (0, i)),
        out_shape=jax.ShapeDtypeStruct((w, n), F32),
        compiler_params=_params("arbitrary"),
        name="dt",
    )(zx, dt_bias.reshape(1, w))


def _scan_kernel(x_ref, b_ref, c_ref, dt_ref, a_ref, s0_ref, *rest, rev, has_prev):
    if has_prev:
        yp_ref, y_ref, sf_ref, st, rows = rest
    else:
        y_ref, sf_ref, st, rows = rest
    step = pl.program_id(0)
    q = CHUNK
    nh = dt_ref.shape[0]
    hpg = nh // N_GROUPS
    half = V7X_LANES // 2

    @pl.when(step == 0)
    def _():
        st[...] = s0_ref[...]

    ri = lax.broadcasted_iota(I32, (q, q), 0)
    ci = lax.broadcasted_iota(I32, (q, q), 1)
    if rev:
        upper, keep = ri >= ci, ri <= ci
    else:
        upper, keep = ri <= ci, ri >= ci
    lm = keep.astype(F32)
    dt = dt_ref[...]
    da = dt * -jnp.exp(a_ref[...])
    acs = jnp.dot(da, upper.astype(F32), precision=HIGHEST, preferred_element_type=F32)
    tot = jnp.sum(da, axis=1, keepdims=True)
    rows[0] = dt
    rows[1] = da
    rows[2] = acs
    rows[3] = dt * jnp.exp(tot - acs)
    rows[4] = jnp.broadcast_to(jnp.exp(tot), (nh, q))

    lane_lo = lax.broadcasted_iota(I32, (q, V7X_LANES), 1) < half
    r2 = lax.broadcasted_iota(I32, (2 * D_STATE, V7X_LANES), 0) < D_STATE
    l2 = lax.broadcasted_iota(I32, (2 * D_STATE, V7X_LANES), 1) < half
    diag = r2 == l2

    def group(g, carry):
        r0 = pl.multiple_of(g * hpg, hpg)
        dtg = rows[0, pl.ds(r0, hpg), :]
        dag = rows[1, pl.ds(r0, hpg), :]
        acg = rows[2, pl.ds(r0, hpg), :]
        wtg = rows[3, pl.ds(r0, hpg), :]
        etg = rows[4, pl.ds(r0, hpg), :]
        c0 = pl.multiple_of(g * D_STATE, D_STATE)
        cg = c_ref[:, pl.ds(c0, D_STATE)]
        bg = b_ref[:, pl.ds(c0, D_STATE)]
        cb = lax.dot_general(cg.astype(BF16), bg.astype(BF16), (((1,), (1,)), ((), ())),
                             preferred_element_type=F32)
        bgt = bg.T
        for k in range(hpg // 2):
            x0 = pl.multiple_of(g * (hpg * HEAD_DIM) + k * V7X_LANES, V7X_LANES)
            x2 = x_ref[:, pl.ds(x0, V7X_LANES)]
            ms, ces, bws, ess = [], [], [], []
            for t in range(2):
                j = 2 * k + t
                acol = jnp.sum(lm * dag[j:j + 1, :], axis=1, keepdims=True)
                dec = jnp.where(keep, jnp.exp(acol - acg[j:j + 1, :]), 0.0)
                ms.append(cb * dec * dtg[j:j + 1, :])
                ces.append(cg * jnp.exp(acol))
                bws.append(bgt * wtg[j:j + 1, :])
                ess.append(jnp.broadcast_to(etg[j:j + 1, :], (D_STATE, V7X_LANES)))
            sp = st[g * (hpg // 2) + k]
            lhs = jnp.concatenate(ms + ces, axis=1).astype(BF16)
            rhs = jnp.concatenate([jnp.where(lane_lo, x2, 0.0), jnp.where(lane_lo, 0.0, x2), sp],
                                  axis=0).astype(BF16)
            yp = jnp.dot(lhs, rhs, preferred_element_type=F32)
            if has_prev:
                yp = yp + yp_ref[:, pl.ds(x0, V7X_LANES)]
            y_ref[:, pl.ds(x0, V7X_LANES)] = yp
            upd = jnp.dot(jnp.concatenate(bws, axis=0).astype(BF16), x2.astype(BF16),
                          preferred_element_type=F32)
            st[g * (hpg // 2) + k] = sp * jnp.concatenate(ess, axis=0) + jnp.where(diag, upd, 0.0)
        return carry

    lax.fori_loop(0, N_GROUPS, group, 0, unroll=True)

    @pl.when(step == pl.num_programs(0) - 1)
    def _():
        sf_ref[...] = st[...]


def ssd_scan_dir(xbc, dt_t, a_row, s0, y_prev, *, direction):
    n = xbc.shape[0]
    nh = dt_t.shape[0] // 2
    d_inner = nh * HEAD_DIM
    gn = N_GROUPS * D_STATE
    assert n % CHUNK == 0 and d_inner % gn == 0
    nc = n // CHUNK
    rev = direction == 1
    cidx = (lambda i: nc - 1 - i) if rev else (lambda i: i)
    ins = [xbc, xbc, xbc, dt_t, a_row, s0]
    specs = [
        pl.BlockSpec((CHUNK, d_inner), lambda i: (cidx(i), 0)),
        pl.BlockSpec((CHUNK, gn), lambda i: (cidx(i), d_inner // gn)),
        pl.BlockSpec((CHUNK, gn), lambda i: (cidx(i), d_inner // gn + 1)),
        pl.BlockSpec((nh, CHUNK), lambda i: (direction, cidx(i))),
        pl.BlockSpec((nh, CHUNK), lambda i: (0, 0)),
        pl.BlockSpec(s0.shape, lambda i: (0, 0, 0)),
    ]
    if y_prev is not None:
        ins.append(y_prev)
        specs.append(pl.BlockSpec((CHUNK, d_inner), lambda i: (cidx(i), 0)))
    return pl.pallas_call(
        functools.partial(_scan_kernel, rev=rev, has_prev=y_prev is not None),
        grid=(nc,),
        in_specs=specs,
        out_specs=[pl.BlockSpec((CHUNK, d_inner), lambda i: (cidx(i), 0)),
                   pl.BlockSpec(s0.shape, lambda i: (0, 0, 0))],
        out_shape=[jax.ShapeDtypeStruct((n, d_inner), F32), jax.ShapeDtypeStruct(s0.shape, F32)],
        scratch_shapes=[pltpu.VMEM(s0.shape, F32), pltpu.VMEM((5, nh, CHUNK), F32)],
        compiler_params=_params("arbitrary"),
        name="scan_bwd" if rev else "scan_fwd",
    )(*ins)


def _gate_kernel(y_ref, xs_ref, z_ref, d_ref, nw_ref, o_ref):
    v = (y_ref[...] + xs_ref[...] * d_ref[...]) * _silu(z_ref[...])
    gw = v.shape[1] // N_GROUPS
    for g in range(N_GROUPS):
        vg = v[:, g * gw:(g + 1) * gw]
        r = lax.rsqrt(jnp.mean(vg * vg, axis=-1, keepdims=True) + EPS)
        o_ref[:, g * gw:(g + 1) * gw] = (vg * r * nw_ref[:, g * gw:(g + 1) * gw]).astype(o_ref.dtype)


def ssd_gate(y, xbc, zx, d_row, norm_w, *, tb):
    n, d_inner = y.shape
    blk = pl.BlockSpec((tb, d_inner), lambda i: (i, 0))
    vec = pl.BlockSpec((1, d_inner), lambda i: (0, 0))
    return pl.pallas_call(
        _gate_kernel,
        grid=(n // tb,),
        in_specs=[blk, blk, blk, vec, vec],
        out_specs=blk,
        out_shape=jax.ShapeDtypeStruct((n, d_inner), BF16),
        compiler_params=_params("arbitrary"),
        name="ssd_gate",
    )(y, xbc, zx, d_row.reshape(1, d_inner), norm_w.reshape(1, d_inner))


def _row_tile(n, pref):
    return pref if n % pref == 0 else n


def ssd_layer(x, ctx_s, mod_l, mod_c, norm1, w_in, conv_w, conv_b, dt_bias, a_log, d_skip, norm_w, w_out, k, ctx_out):
    nh = a_log.shape[1]
    d_inner = nh * HEAD_DIM
    gn = N_GROUPS * D_STATE
    a_rows = jnp.broadcast_to(a_log.astype(F32)[:, :, None], (2, nh, CHUNK))
    d_row = jnp.repeat(d_skip.astype(F32), HEAD_DIM)
    in_tn = w_in.shape[-1] // 9

    def project(v, mod, width):
        n = v.shape[0]
        zx = matmul(v, w_in, w_index=k, mod=(norm1, mod[1], mod[0]), tm=_row_tile(n, 1024), tn=in_tn,
                    name="ssd_in")
        xbc = conv_silu(zx, conv_w, conv_b, width=width, col0=d_inner, tb=_row_tile(n, 1024))
        dt_t = dt_rows(zx, dt_bias, col0=2 * d_inner + 2 * gn, tb=_row_tile(n, 512))
        return zx, xbc, dt_t

    zx_c, xbc_c, dt_c = project(ctx_s, mod_c, ctx_s.shape[0])
    zx_l, xbc_l, dt_l = project(x, mod_l, GRID_W)
    zero = jnp.zeros((nh // 2, 2 * D_STATE, V7X_LANES), F32)
    yc, s_f = ssd_scan_dir(xbc_c, dt_c, a_rows[0], zero, None, direction=0)
    yl, _ = ssd_scan_dir(xbc_l, dt_l, a_rows[0], s_f, None, direction=0)
    yc, s_b = ssd_scan_dir(xbc_c, dt_c, a_rows[1], zero, yc, direction=1)
    yl, _ = ssd_scan_dir(xbc_l, dt_l, a_rows[1], s_b, yl, direction=1)

    def output(v, y, xbc, zx, gate):
        n = v.shape[0]
        yn = ssd_gate(y, xbc, zx, d_row, norm_w, tb=_row_tile(n, 256))
        return matmul(yn, w_out, w_index=k, res=v, gate=gate, tm=_row_tile(n, 512), tn=1024, name="ssd_out")

    x_new = output(x, yl, xbc_l, zx_l, mod_l[2])
    ctx_new = output(ctx_s, yc, xbc_c, zx_c, mod_c[2]) if ctx_out else None
    return x_new, ctx_new


def _dft_tables(n):
    j = jnp.arange(n, dtype=I32)
    ang = ((j[:, None] * j[None, :]) % n).astype(F32) * (2.0 * math.pi / n)
    return jnp.cos(ang), jnp.sin(ang)


def _fchan_kernel(x_ref, gain_ref, sc_ref, sh_ref, cs_ref, o_ref):
    h = _rms_mod(x_ref[...], gain_ref[...], sc_ref[...], sh_ref[...]).astype(BF16)
    gw = cs_ref.shape[0]
    d = h.shape[1]
    for g in range(d // gw):
        r = jnp.dot(h[:, g * gw:(g + 1) * gw], cs_ref[...], preferred_element_type=F32)
        o_ref[:, g * gw:(g + 1) * gw] = r[:, :gw]
        o_ref[:, d + g * gw:d + (g + 1) * gw] = r[:, gw:]


def _f1_kernel(r_ref, i_ref, k1_ref, tc_ref, ts_ref, yr_ref, yi_ref):
    la, sub, cw = r_ref.shape
    rows = la * sub
    xx = jnp.concatenate([r_ref[...].reshape(rows, cw), i_ref[...].reshape(rows, cw)], axis=0).astype(BF16)
    y = jnp.dot(k1_ref[...], xx, preferred_element_type=F32)
    yr, yi = y[:rows], y[rows:]
    c, s = tc_ref[...], ts_ref[...]
    yr_ref[...] = (c * yr + s * yi).reshape(la, sub, cw)
    yi_ref[...] = (c * yi - s * yr).reshape(la, sub, cw)


def _f2_kernel(r_ref, i_ref, k2_ref, o_ref):
    sub, lb, cw = r_ref.shape
    xx = jnp.concatenate([r_ref[...].reshape(sub * lb, cw), i_ref[...].reshape(sub * lb, cw)],
                         axis=0).astype(BF16)
    o_ref[...] = jnp.dot(k2_ref[...], xx, preferred_element_type=F32).reshape(lb, sub, cw)


def fnet_layer(x, mod, norm1, w_out, k, *, cw=512):
    l, d = x.shape
    sub = V7X_SUBLANES
    la = l // 128 if l >= 1024 else sub
    lb = l // la
    gw = d // FNET_GROUPS
    assert la % sub == 0 and lb % sub == 0 and d % cw == 0
    shift, scale, gate = mod
    cc, sc = _dft_tables(gw)
    norm = 1.0 / math.sqrt(l * gw)
    cs = (jnp.concatenate([cc, -sc], axis=1) * norm).astype(BF16)
    tm = _row_tile(l, 512)
    vec = pl.BlockSpec((1, d), lambda i: (0, 0))
    g2 = pl.pallas_call(
        _fchan_kernel,
        grid=(l // tm,),
        in_specs=[pl.BlockSpec((tm, d), lambda i: (i, 0)), vec, vec, vec, pl.BlockSpec((gw, 2 * gw), lambda i: (0, 0))],
        out_specs=pl.BlockSpec((tm, 2 * d), lambda i: (i, 0)),
        out_shape=jax.ShapeDtypeStruct((l, 2 * d), F32),
        compiler_params=_params("arbitrary"),
        name="fnet_chan",
    )(x, norm1.reshape(1, d), scale.reshape(1, d), shift.reshape(1, d), cs)
    ca, sa = _dft_tables(la)
    eye = jnp.eye(sub, dtype=F32)
    k1 = jnp.concatenate([jnp.concatenate([jnp.kron(ca, eye), jnp.kron(sa, eye)], axis=1),
                          jnp.concatenate([jnp.kron(-sa, eye), jnp.kron(ca, eye)], axis=1)], axis=0).astype(BF16)
    bi = jnp.arange(lb, dtype=I32).reshape(lb // sub, 1, sub)
    di = jnp.arange(la, dtype=I32).reshape(1, la, 1)
    ang = ((bi * di) % l).astype(F32) * (2.0 * math.pi / l)
    tc = jnp.cos(ang).reshape(lb // sub, la * sub, 1)
    ts = jnp.sin(ang).reshape(lb // sub, la * sub, 1)
    g3 = g2.reshape(la, lb, 2 * d)
    ncb = d // cw
    blk1 = lambda off: pl.BlockSpec((la, sub, cw), lambda b, c: (0, b, off + c))
    twb = pl.BlockSpec((None, la * sub, 1), lambda b, c: (b, 0, 0))
    yr3, yi3 = pl.pallas_call(
        _f1_kernel,
        grid=(lb // sub, ncb),
        in_specs=[blk1(0), blk1(ncb), pl.BlockSpec(k1.shape, lambda b, c: (0, 0)), twb, twb],
        out_specs=[blk1(0), blk1(0)],
        out_shape=[jax.ShapeDtypeStruct((la, lb, d), F32)] * 2,
        compiler_params=_params("arbitrary", "arbitrary"),
        name="fnet_stage1",
    )(g3, g3, k1, tc, ts)
    cb_, sb_ = _dft_tables(lb)
    k2 = jnp.concatenate([jnp.einsum("cb,de->cdeb", cb_, eye).reshape(lb * sub, sub * lb),
                          jnp.einsum("cb,de->cdeb", sb_, eye).reshape(lb * sub, sub * lb)], axis=1).astype(BF16)
    blk2 = pl.BlockSpec((sub, lb, cw), lambda dd, c: (dd, 0, c))
    f3 = pl.pallas_call(
        _f2_kernel,
        grid=(la // sub, ncb),
        in_specs=[blk2, blk2, pl.BlockSpec(k2.shape, lambda dd, c: (0, 0))],
        out_specs=pl.BlockSpec((lb, sub, cw), lambda dd, c: (0, dd, c)),
        out_shape=jax.ShapeDtypeStruct((lb, la, d), F32),
        compiler_params=_params("arbitrary", "arbitrary"),
        name="fnet_stage2",
    )(yr3, yi3, k2)
    return matmul(f3.reshape(l, d), w_out, w_index=k, res=x, gate=gate, tm=_row_tile(l, 512), tn=1024,
                  name="fnet_out")


ROUTE_S = 128
ROUTE_TOKENS = ROUTE_S * ROUTE_S


def _moe_prep_kernel(x_ref, gain_ref, sc_ref, sh_ref, wr_ref, h_ref, lg_ref):
    h = _rms_mod(x_ref[...], gain_ref[...], sc_ref[...], sh_ref[...])
    tb = h.shape[0]
    for j in range(TOK_ROWS):
        h_ref[pl.ds(j, tb, stride=TOK_ROWS), :] = h[:, j * V7X_LANES:(j + 1) * V7X_LANES]
    lg_ref[...] = lax.dot_general(wr_ref[...], h, (((1,), (1,)), ((), ())), precision=HIGHEST,
                                  preferred_element_type=F32)


def moe_prep(x, norm2, scale, shift, w_router, *, tb):
    n, d = x.shape
    e = w_router.shape[1]
    assert d == TOK_ROWS * V7X_LANES and n % tb == 0
    vec = pl.BlockSpec((1, d), lambda i: (0, 0))
    return pl.pallas_call(
        _moe_prep_kernel,
        grid=(n // tb,),
        in_specs=[pl.BlockSpec((tb, d), lambda i: (i, 0)), vec, vec, vec, pl.BlockSpec((e, d), lambda i: (0, 0))],
        out_specs=[pl.BlockSpec((tb * TOK_ROWS, V7X_LANES), lambda i: (i, 0)), pl.BlockSpec((e, tb), lambda i: (0, i))],
        out_shape=[jax.ShapeDtypeStruct((n * TOK_ROWS, V7X_LANES), F32), jax.ShapeDtypeStruct((e, n), F32)],
        compiler_params=_params("arbitrary"),
        name="moe_prep",
    )(x, norm2.reshape(1, d), scale.reshape(1, d), shift.reshape(1, d), w_router.T)


def _tok_cumsum(m, lincl_bf16, ustrict):
    wc = jnp.dot(lincl_bf16, m.astype(BF16), preferred_element_type=F32)
    coltot = wc[ROUTE_S - 1:ROUTE_S, :]
    colpref = jnp.dot(jnp.broadcast_to(coltot, (V7X_SUBLANES, ROUTE_S)), ustrict, precision=HIGHEST,
                      preferred_element_type=F32)[0:1]
    return wc, coltot, colpref


def _route_kernel(lg_ref, idx_ref, q_ref, g_ref, off_ref, cnt_ref, aff_scr, bits_scr, sel_scr, qt_scr, *,
                  n_valid, cap, pc):
    ne = lg_ref.shape[0]
    s_ = ROUTE_S
    c_pad = idx_ref.shape[2]
    si = lax.broadcasted_iota(I32, (s_, s_), 0)
    ji = lax.broadcasted_iota(I32, (s_, s_), 1)
    valid = ji * s_ + si < n_valid
    lincl = (ji <= si).astype(BF16)
    ustrict = (si < ji).astype(F32)
    ones = jnp.ones((s_, s_), BF16)

    ls = [lg_ref[e] for e in range(ne)]
    mx = functools.reduce(jnp.maximum, ls)
    ex = [jnp.exp(l - mx) for l in ls]
    den = functools.reduce(jnp.add, ex)
    for e in range(ne):
        aff = jnp.where(valid, ex[e] / den, -1.0)
        aff_scr[e] = aff
        bits_scr[e] = pltpu.bitcast(aff, I32)

    def bit_step(i, ts):
        bit = jnp.left_shift(jnp.int32(1), 30 - i)
        out = []
        for e in range(ne):
            cand = ts[e] | bit
            ge = (bits_scr[e] >= cand).astype(F32)
            c1 = jnp.sum(jnp.sum(ge, axis=0, keepdims=True), axis=1, keepdims=True)
            out.append(jnp.where(c1 >= cap, cand, ts[e]))
        return tuple(out)

    ts = lax.fori_loop(0, 31, bit_step, tuple(jnp.zeros((1, 1), I32) for _ in range(ne)))

    cnt = jnp.zeros((s_, s_), F32)
    for e in range(ne):
        bits = bits_scr[e]
        gt = bits > ts[e]
        eq = (bits == ts[e]).astype(F32)
        n_gt = jnp.sum(jnp.sum(gt.astype(F32), axis=0, keepdims=True), axis=1, keepdims=True)
        wc, _, colpref = _tok_cumsum(eq, lincl, ustrict)
        rank_eq = wc + colpref - eq
        sel = jnp.where(gt | ((eq > 0.0) & (rank_eq < cap - n_gt)), 1.0, 0.0)
        sel_scr[e] = sel
        qt_scr[e] = cnt
        cnt = cnt + sel
    wc, _, colpref = _tok_cumsum(cnt, lincl, ustrict)
    offs = wc + colpref - cnt
    off_ref[...] = offs
    cnt_ref[...] = cnt

    def per_expert(e, carry):
        m = sel_scr[e]
        qt = qt_scr[e] + offs
        aff = aff_scr[e]
        wc, coltot, colpref = _tok_cumsum(m, lincl, ustrict)
        wc_b = wc.astype(BF16)
        colcum = jnp.dot((lincl.astype(F32) * coltot).astype(BF16), ones, preferred_element_type=F32)
        colpref8 = jnp.broadcast_to(colpref, (V7X_SUBLANES, s_))
        sub = lax.broadcasted_iota(I32, (s_, pc), 0).astype(F32)
        reps = pc // s_
        colcum_t = jnp.concatenate([colcum] * reps, axis=1) if reps > 1 else colcum
        for c0 in range(0, c_pad, pc):
            p = (lax.broadcasted_iota(I32, (s_, pc), 1) + c0).astype(F32)
            p_row = p[0:1]
            blk = jnp.sum((colcum_t <= p).astype(F32), axis=0, keepdims=True)
            oh_j = (sub == blk).astype(F32)
            colvec = jnp.dot(wc_b, oh_j.astype(BF16), preferred_element_type=F32)
            cp = jnp.dot(colpref8, oh_j, precision=HIGHEST, preferred_element_type=F32)[0:1]
            s_idx = jnp.sum((colvec <= p_row - cp).astype(F32), axis=0, keepdims=True)
            oh_s = (sub == s_idx).astype(F32)
            gv = jnp.sum(oh_s * jnp.dot(aff, oh_j, precision=HIGHEST, preferred_element_type=F32),
                         axis=0, keepdims=True)
            qv = jnp.sum(oh_s * jnp.dot(qt, oh_j, precision=HIGHEST, preferred_element_type=F32),
                         axis=0, keepdims=True)
            live = p_row < cap
            idx_ref[e, :, c0:c0 + pc] = jnp.where(live, blk * s_ + s_idx, 0.0).astype(I32)
            q_ref[e, :, c0:c0 + pc] = jnp.where(live, qv, 0.0).astype(I32)
            g_ref[e, :, c0:c0 + pc] = jnp.where(live, gv, 0.0)
        return carry

    lax.fori_loop(0, ne, per_expert, 0)


def moe_route(logits_t, *, cap):
    ne, n = logits_t.shape
    assert n <= ROUTE_TOKENS and cap <= n
    s_ = ROUTE_S
    c_pad = max(cap, s_)
    pc = min(c_pad, 2 * s_)
    assert c_pad % pc == 0
    lg = jnp.pad(logits_t, ((0, 0), (0, ROUTE_TOKENS - n))).reshape(ne, s_, s_).transpose(0, 2, 1)
    full = lambda shape: pl.BlockSpec(shape, lambda i: (0,) * len(shape))
    slot = jax.ShapeDtypeStruct((ne, 1, c_pad), I32)
    idx, q, g, offs, cnt = pl.pallas_call(
        functools.partial(_route_kernel, n_valid=n, cap=cap, pc=pc),
        grid=(1,),
        in_specs=[full((ne, s_, s_))],
        out_specs=[full((ne, 1, c_pad))] * 3 + [full((s_, s_))] * 2,
        out_shape=[slot, slot, jax.ShapeDtypeStruct((ne, 1, c_pad), F32),
                   jax.ShapeDtypeStruct((s_, s_), F32), jax.ShapeDtypeStruct((s_, s_), F32)],
        scratch_shapes=[pltpu.VMEM((ne, s_, s_), F32), pltpu.VMEM((ne, s_, s_), I32),
                        pltpu.VMEM((ne, s_, s_), F32), pltpu.VMEM((ne, s_, s_), F32)],
        compiler_params=_params("arbitrary"),
        name="moe_route",
    )(lg)
    tok = lambda a: a.T.reshape(-1)[:n]
    return idx[:, 0, :cap], q[:, 0, :cap], g[:, 0, :cap], tok(offs), tok(cnt)


def _ffn_kernel(idx_ref, q_ref, g_ref, h_hbm, wg_ref, wu_ref, wd_ref, z_hbm, ga, gb, sa, sb, sems):
    step = pl.program_id(0)
    last = pl.num_programs(0) - 1
    cbh = ga.shape[0] // TOK_ROWS
    total = idx_ref.shape[0]
    base = step * (2 * cbh)
    srows = lambda p: pl.ds(p * TOK_ROWS, TOK_ROWS)
    drows = lambda r: pl.ds(pl.multiple_of(r * TOK_ROWS, TOK_ROWS), TOK_ROWS)
    G_A, G_B, S_A, S_B = range(4)

    def gather_start(buf, sem, slot0):
        for p in range(cbh):
            pltpu.make_async_copy(h_hbm.at[drows(idx_ref[slot0 + p]), :], buf.at[srows(p), :], sems.at[sem]).start()

    def gather_wait(buf, sem):
        for p in range(cbh):
            pltpu.make_async_copy(h_hbm.at[srows(0), :], buf.at[srows(p), :], sems.at[sem]).wait()

    def scatter_start(buf, sem, slot0):
        for p in range(cbh):
            pltpu.make_async_copy(buf.at[srows(p), :], z_hbm.at[drows(q_ref[slot0 + p]), :],
                                  sems.at[sem]).start(priority=p % 2)

    def scatter_wait(buf, sem):
        for p in range(cbh):
            pltpu.make_async_copy(buf.at[srows(p), :], z_hbm.at[srows(0), :], sems.at[sem]).wait()

    def compute(gbuf, sbuf, g):
        x = jnp.concatenate([gbuf[pl.ds(j, cbh, stride=TOK_ROWS), :] for j in range(TOK_ROWS)],
                            axis=1).astype(BF16)
        a = jnp.dot(x, wg_ref[...], preferred_element_type=F32)
        u = jnp.dot(x, wu_ref[...], preferred_element_type=F32)
        y = jnp.dot((_silu(a) * u).astype(BF16), wd_ref[...], preferred_element_type=F32) * g
        for j in range(TOK_ROWS):
            sbuf[pl.ds(j, cbh, stride=TOK_ROWS), :] = y[:, j * V7X_LANES:(j + 1) * V7X_LANES]

    @pl.when(step == 0)
    def _():
        gather_start(ga, G_A, 0)

    gather_wait(ga, G_A)

    @pl.when(step > 0)
    def _():
        scatter_wait(sa, S_A)

    gather_start(gb, G_B, base + cbh)
    compute(ga, sa, g_ref[0:cbh, :])
    scatter_start(sa, S_A, base)
    gather_wait(gb, G_B)

    @pl.when(step > 0)
    def _():
        scatter_wait(sb, S_B)

    gather_start(ga, G_A, jnp.minimum(base + 2 * cbh, total - cbh))
    compute(gb, sb, g_ref[cbh:2 * cbh, :])
    scatter_start(sb, S_B, base + cbh)

    @pl.when(step == last)
    def _():
        gather_wait(ga, G_A)
        scatter_wait(sa, S_A)
        scatter_wait(sb, S_B)


def moe_ffn(h_rows, idx, q, g, w_gate, w_up, w_down, layer, *, cbh):
    ne, cap = idx.shape
    d, f = w_gate.shape[2:]
    sb_rows = 2 * cbh
    assert cap % sb_rows == 0
    spe = cap // sb_rows
    wmap = lambda s, *_: (layer, s // spe, 0, 0)
    buf = pltpu.VMEM((cbh * TOK_ROWS, V7X_LANES), F32)
    return pl.pallas_call(
        _ffn_kernel,
        grid_spec=pltpu.PrefetchScalarGridSpec(
            num_scalar_prefetch=2,
            grid=(ne * spe,),
            in_specs=[
                pl.BlockSpec((sb_rows, 1), lambda s, *_: (s, 0)),
                pl.BlockSpec(memory_space=pl.ANY),
                pl.BlockSpec((None, None, d, f), wmap),
                pl.BlockSpec((None, None, d, f), wmap),
                pl.BlockSpec((None, None, f, d), wmap),
            ],
            out_specs=pl.BlockSpec(memory_space=pl.ANY),
            scratch_shapes=[buf, buf, buf, buf, pltpu.SemaphoreType.DMA((4,))],
        ),
        out_shape=jax.ShapeDtypeStruct((ne * cap * TOK_ROWS, V7X_LANES), F32),
        compiler_params=_params("arbitrary"),
        name="moe_ffn",
    )(idx.reshape(-1), q.reshape(-1), g.reshape(ne * cap, 1), h_rows, w_gate, w_up, w_down)


def _combine_kernel(boff_ref, x_ref, gate_ref, off_ref, cnt_ref, z_hbm, o_ref, z0, z1, sems, *, ch, p_total):
    i = pl.program_id(0)
    start = boff_ref[i]
    end = boff_ref[i + 1]
    nch = (end - start + ch - 1) // ch
    lo = off_ref[...]
    hi = lo + cnt_ref[...]
    o_ref[...] = jnp.zeros(o_ref.shape, F32)
    zbufs = (z0, z1)

    def row_start(c, first_row=None):
        first_row = start if first_row is None else first_row
        return jnp.minimum(first_row + c * ch, p_total - ch)

    def copy(c, slot, first_row=None):
        src = z_hbm.at[pl.ds(pl.multiple_of(row_start(c, first_row) * TOK_ROWS, TOK_ROWS), ch * TOK_ROWS), :]
        return pltpu.make_async_copy(src, zbufs[slot], sems.at[slot])

    def process(c, slot):
        copy(c, slot).wait()

        @pl.when(c + 1 < nch)
        def _():
            copy(c + 1, 1 - slot).start()

        first = start + c * ch
        qabs = row_start(c) + lax.broadcasted_iota(I32, (1, ch), 1)
        qf = qabs.astype(F32)
        seg = ((qf >= lo) & (qf < hi) & (qabs >= first)).astype(BF16)
        zb = zbufs[slot]
        z = jnp.concatenate([zb[pl.ds(j, ch, stride=TOK_ROWS), :] for j in range(TOK_ROWS)], axis=1)
        zh = z.astype(BF16)
        zl = (z - zh.astype(F32)).astype(BF16)
        o_ref[...] += (jnp.dot(seg, zh, preferred_element_type=F32) + jnp.dot(seg, zl, preferred_element_type=F32))

    @pl.when((i == 0) & (nch > 0))
    def _():
        copy(0, 0).start()

    def pair(k, carry):
        process(2 * k, 0)

        @pl.when(2 * k + 1 < nch)
        def _():
            process(2 * k + 1, 1)

        return carry

    lax.fori_loop(0, (nch + 1) // 2, pair, 0)

    @pl.when(i + 1 < pl.num_programs(0))
    def _():
        @pl.when(boff_ref[i + 2] > end)
        def _():
            copy(0, 0, first_row=end).start()

    o_ref[...] = x_ref[...] + gate_ref[...] * o_ref[...]


def moe_combine(x, gate, z_rows, offs, cnt, *, tb, ch=256):
    n, d = x.shape
    p_total = z_rows.shape[0] // TOK_ROWS
    assert n % tb == 0 and p_total >= ch
    boff = jnp.concatenate([offs[::tb], jnp.full((1,), p_total, F32)]).astype(I32)
    col = pl.BlockSpec((tb, 1), lambda i, s: (i, 0))
    return pl.pallas_call(
        functools.partial(_combine_kernel, ch=ch, p_total=p_total),
        grid_spec=pltpu.PrefetchScalarGridSpec(
            num_scalar_prefetch=1,
            grid=(n // tb,),
            in_specs=[pl.BlockSpec((tb, d), lambda i, s: (i, 0)), pl.BlockSpec((1, d), lambda i, s: (0, 0)),
                      col, col, pl.BlockSpec(memory_space=pl.ANY)],
            out_specs=pl.BlockSpec((tb, d), lambda i, s: (i, 0)),
            scratch_shapes=[pltpu.VMEM((ch * TOK_ROWS, V7X_LANES), F32), pltpu.VMEM((ch * TOK_ROWS, V7X_LANES), F32),
                            pltpu.SemaphoreType.DMA((2,))],
        ),
        out_shape=jax.ShapeDtypeStruct((n, d), F32),
        compiler_params=_params("arbitrary"),
        name="moe_combine",
    )(boff, x, gate.reshape(1, d), offs.reshape(n, 1), cnt.reshape(n, 1), z_rows)


def moe_layer(x, mod, norm2, w_router, w_gate, w_up, w_down, layer):
    n = x.shape[0]
    ne = w_router.shape[1]
    cap = (CAPACITY_FACTOR * n) // ne
    shift, scale, gate = mod
    h_rows, logits_t = moe_prep(x, norm2, scale, shift, w_router, tb=_row_tile(n, 256))
    idx, q, g, offs, cnt = moe_route(logits_t, cap=cap)
    z_rows = moe_ffn(h_rows, idx, q, g, w_gate, w_up, w_down, layer, cbh=min(cap // 2, 256))
    return moe_combine(x, gate, z_rows, offs, cnt, tb=_row_tile(n, 512))


def _final_norm_kernel(x_ref, w_ref, o_ref):
    x = x_ref[...]
    o_ref[...] = x * lax.rsqrt(jnp.mean(x * x, axis=-1, keepdims=True) + EPS) * w_ref[...]


def final_norm(x, w, *, tb):
    n, d = x.shape
    return pl.pallas_call(
        _final_norm_kernel,
        grid=(n // tb,),
        in_specs=[pl.BlockSpec((tb, d), lambda i: (i, 0)), pl.BlockSpec((1, d), lambda i: (0, 0))],
        out_specs=pl.BlockSpec((tb, d), lambda i: (i, 0)),
        out_shape=jax.ShapeDtypeStruct((n, d), F32),
        compiler_params=_params("arbitrary"),
        name="final_norm",
    )(x, w.reshape(1, d))


def kernel(x, c, ctx, c_ctx, ada_w, ada_b, norm1_w, norm2_w, final_norm_w, ssd_w_in, ssd_conv_w, ssd_conv_b,
           ssd_dt_bias, ssd_a_log, ssd_d, ssd_norm_w, ssd_w_out, fnet_w_out, moe_w_router, moe_w_gate, moe_w_up,
           moe_w_down):
    depth = ada_w.shape[0]
    d = x.shape[-1]
    assert x.shape[0] == 1, "one sample per call"
    xs, cs = x[0], ctx[0]
    ada = ada_all(c, c_ctx, ada_w, ada_b)
    moe_w = (moe_w_gate.astype(BF16), moe_w_up.astype(BF16), moe_w_down.astype(BF16))
    w_in_b, w_out_b, w_fnet_b = ssd_w_in.astype(BF16), ssd_w_out.astype(BF16), fnet_w_out.astype(BF16)
    for i in range(depth):
        k = i // 2
        is_ssd = i % 2 == 0
        ctx_later = any(j % 2 == 0 for j in range(i + 1, depth))
        sh1, sc1, g1, sh2, sc2, g2 = (ada[i, 0, m * d:(m + 1) * d] for m in range(6))
        csh1, csc1, cg1, csh2, csc2, cg2 = (ada[i, 1, m * d:(m + 1) * d] for m in range(6))
        if is_ssd:
            xs, cs_new = ssd_layer(xs, cs, (sh1, sc1, g1), (csh1, csc1, cg1), norm1_w[i], w_in_b, ssd_conv_w[k],
                                   ssd_conv_b[k], ssd_dt_bias[k], ssd_a_log[k], ssd_d[k], ssd_norm_w[k],
                                   w_out_b, k, ctx_later)
        else:
            xs = fnet_layer(xs, (sh1, sc1, g1), norm1_w[i], w_fnet_b, k)
            cs_new = fnet_layer(cs, (csh1, csc1, cg1), norm1_w[i], w_fnet_b, k) if ctx_later else None
        xs = moe_layer(xs, (sh2, sc2, g2), norm2_w[i], moe_w_router[i], *moe_w, i)
        if ctx_later:
            cs = moe_layer(cs_new, (csh2, csc2, cg2), norm2_w[i], moe_w_router[i], *moe_w, i)
    return final_norm(xs, final_norm_w, tb=512)[None]
```

```python
import functools
import math

import jax
import jax.numpy as jnp
from jax import lax
from jax.experimental import pallas as pl
from jax.experimental.pallas import tpu as pltpu

F32 = jnp.float32
BF16 = jnp.bfloat16
I32 = jnp.int32
HIGHEST = lax.Precision.HIGHEST

V7X_LANES = 128
V7X_SUBLANES = 8
V7X_VMEM_BYTES = 64 * 1024 * 1024
VMEM_LIMIT = V7X_VMEM_BYTES - 8 * 1024 * 1024

GRID_W = 64
HEAD_DIM = 64
N_GROUPS = 8
D_STATE = 128
CHUNK = 128
FNET_GROUPS = 8
N_EXPERTS = 16
CAPACITY_FACTOR = 2
EPS = 1e-6
TOK_ROWS = 16


def _params(*sem):
    return pltpu.CompilerParams(dimension_semantics=sem, vmem_limit_bytes=VMEM_LIMIT)


def _rms_mod(x, gain, scale, shift):
    y = x * lax.rsqrt(jnp.mean(x * x, axis=-1, keepdims=True) + EPS)
    return y * gain * (1.0 + scale) + shift


def _silu(x):
    return x * jax.nn.sigmoid(x)


def _ada_kernel(c_ref, w_ref, b_ref, o_ref):
    w = w_ref[...]
    rows = []
    for r in range(2):
        s = _silu(c_ref[r])
        rows.append(jnp.sum(s * w, axis=0, keepdims=True) + b_ref[...])
    rows.append(jnp.zeros((V7X_SUBLANES - 2, w.shape[1]), F32))
    o_ref[...] = jnp.concatenate(rows, axis=0)


def ada_all(c, c_ctx, ada_w, ada_b):
    depth, k, n6 = ada_w.shape
    tn = n6 // 8
    cond = jnp.stack([c[0], c_ctx], axis=0)[:, :, None]
    out = pl.pallas_call(
        _ada_kernel,
        grid=(depth, n6 // tn),
        in_specs=[
            pl.BlockSpec((2, k, 1), lambda l, n: (0, 0, 0)),
            pl.BlockSpec((None, k, tn), lambda l, n: (l, 0, n)),
            pl.BlockSpec((None, 1, tn), lambda l, n: (l, 0, n)),
        ],
        out_specs=pl.BlockSpec((None, V7X_SUBLANES, tn), lambda l, n: (l, 0, n)),
        out_shape=jax.ShapeDtypeStruct((depth, V7X_SUBLANES, n6), F32),
        compiler_params=_params("arbitrary", "arbitrary"),
        name="ada",
    )(cond, ada_w, ada_b[:, None, :])
    return out


def _mm_kernel(*refs, prologue, epilogue):
    it = iter(refs)
    a_ref = next(it)
    if prologue:
        gain_ref, sc_ref, sh_ref = next(it), next(it), next(it)
    w_ref = next(it)
    if epilogue:
        res_ref, gate_ref = next(it), next(it)
    o_ref = next(it)
    a_scr = next(it, None)

    if a_scr is None:
        a_bf16 = a_ref[...]
    else:
        @pl.when(pl.program_id(1) == 0)
        def _():
            a = a_ref[...]
            if prologue:
                a = _rms_mod(a, gain_ref[...], sc_ref[...], sh_ref[...])
            a_scr[...] = a.astype(BF16)

        a_bf16 = a_scr[...]
    acc = jnp.dot(a_bf16, w_ref[...], preferred_element_type=F32)
    if epilogue:
        acc = res_ref[...] + gate_ref[...] * acc
    o_ref[...] = acc.astype(o_ref.dtype)


def matmul(a, w_bf16, *, w_index=None, mod=None, res=None, gate=None, tm, tn, out_dtype=F32, name="mm"):
    m, k = a.shape
    n = w_bf16.shape[-1]
    assert m % tm == 0 and n % tn == 0
    use_scr = mod is not None or a.dtype != BF16
    row = lambda i, j: (0, 0)
    ins, specs = [a], [pl.BlockSpec((tm, k), lambda i, j: (i, 0))]
    if mod is not None:
        for v in mod:
            ins.append(v.reshape(1, k))
            specs.append(pl.BlockSpec((1, k), row))
    ins.append(w_bf16)
    if w_index is None:
        specs.append(pl.BlockSpec((k, tn), lambda i, j: (0, j)))
    else:
        specs.append(pl.BlockSpec((None, k, tn), lambda i, j: (w_index, 0, j)))
    if res is not None:
        ins += [res, gate.reshape(1, n)]
        specs += [pl.BlockSpec((tm, tn), lambda i, j: (i, j)), pl.BlockSpec((1, tn), lambda i, j: (0, j))]
    return pl.pallas_call(
        functools.partial(_mm_kernel, prologue=mod is not None, epilogue=res is not None),
        grid=(m // tm, n // tn),
        in_specs=specs,
        out_specs=pl.BlockSpec((tm, tn), lambda i, j: (i, j)),
        out_shape=jax.ShapeDtypeStruct((m, n), out_dtype),
        scratch_shapes=[pltpu.VMEM((tm, k), BF16)] if use_scr else [],
        compiler_params=_params("arbitrary", "arbitrary"),
        name=name,
    )(*ins)


def _conv_kernel(prev_ref, cur_ref, next_ref, w_ref, b_ref, o_ref, *, width):
    i = pl.program_id(0)
    last = pl.num_programs(0) - 1
    cur = cur_ref[...]
    tb, cb = cur.shape
    prev = jnp.where(i > 0, prev_ref[...], 0.0)
    nxt = jnp.where(i < last, next_ref[...], 0.0)
    if tb > width:
        up = jnp.concatenate([prev, cur[: tb - width]], axis=0)
        dn = jnp.concatenate([cur[width:], nxt], axis=0)
    else:
        up, dn = prev, nxt
    w = w_ref[...]

    def tap(dw):
        return up * w[dw:dw + 1] + cur * w[3 + dw:4 + dw] + dn * w[6 + dw:7 + dw]

    col = lax.broadcasted_iota(I32, (tb, cb), 0) & (width - 1)
    left = jnp.where(col == 0, 0.0, pltpu.roll(tap(0), 1, axis=0))
    right = jnp.where(col == width - 1, 0.0, pltpu.roll(tap(2), tb - 1, axis=0))
    o_ref[...] = _silu(left + tap(1) + right + b_ref[...])


def conv_silu(zx, conv_w, conv_b, *, width, col0, tb, cb=512):
    n = zx.shape[0]
    c = conv_w.shape[-1]
    assert width & (width - 1) == 0 and tb % width == 0 and n % tb == 0 and c % cb == 0 and col0 % cb == 0
    cblk0 = col0 // cb
    per = tb // width
    nrow = n // width
    return pl.pallas_call(
        functools.partial(_conv_kernel, width=width),
        grid=(n // tb, c // cb),
        in_specs=[
            pl.BlockSpec((width, cb), lambda i, j: (jnp.maximum(i * per - 1, 0), cblk0 + j)),
            pl.BlockSpec((tb, cb), lambda i, j: (i, cblk0 + j)),
            pl.BlockSpec((width, cb), lambda i, j: (jnp.minimum((i + 1) * per, nrow - 1), cblk0 + j)),
            pl.BlockSpec((9, cb), lambda i, j: (0, j)),
            pl.BlockSpec((1, cb), lambda i, j: (0, j)),
        ],
        out_specs=pl.BlockSpec((tb, cb), lambda i, j: (i, j)),
        out_shape=jax.ShapeDtypeStruct((n, c), F32),
        compiler_params=_params("arbitrary", "arbitrary"),
        name="conv",
    )(zx, zx, zx, conv_w.reshape(9, c), conv_b.reshape(1, c))


def _dt_kernel(x_ref, b_ref, o_ref):
    v = x_ref[...] + b_ref[...]
    sp = jnp.maximum(v, 0.0) + jnp.log1p(jnp.exp(-jnp.abs(v)))
    o_ref[...] = sp.T


def dt_rows(zx, dt_bias, *, col0, tb):
    n = zx.shape[0]
    w = dt_bias.size
    assert col0 % w == 0 and n % tb == 0
    return pl.pallas_call(
        _dt_kernel,
        grid=(n // tb,),
        in_specs=[pl.BlockSpec((tb, w), lambda i: (i, col0 // w)), pl.BlockSpec((1, w), lambda i: (0, 0))],
        out_specs=pl.BlockSpec((w, tb), lambda i: (0, i)),
        out_shape=jax.ShapeDtypeStruct((w, n), F32),
        compiler_params=_params("arbitrary"),
        name="dt",
    )(zx, dt_bias.reshape(1, w))


def _scan_kernel(x_ref, b_ref, c_ref, dt_ref, a_ref, s0_ref, *rest, rev, has_prev, gated):
    rest = list(rest)
    yp_ref = rest.pop(0) if has_prev else None
    z_ref, dsk_ref, nw_ref = (rest.pop(0), rest.pop(0), rest.pop(0)) if gated else (None, None, None)
    y_ref, sf_ref, st, rows = rest
    step = pl.program_id(0)
    q = CHUNK
    nh = dt_ref.shape[0]
    hpg = nh // N_GROUPS
    half = V7X_LANES // 2

    @pl.when(step == 0)
    def _():
        st[...] = s0_ref[...]

    ri = lax.broadcasted_iota(I32, (q, q), 0)
    ci = lax.broadcasted_iota(I32, (q, q), 1)
    if rev:
        upper, keep = ri >= ci, ri <= ci
    else:
        upper, keep = ri <= ci, ri >= ci
    lm = keep.astype(F32)
    dt = dt_ref[...]
    da = dt * -jnp.exp(a_ref[...])
    acs = jnp.dot(da, upper.astype(F32), precision=HIGHEST, preferred_element_type=F32)
    tot = jnp.sum(da, axis=1, keepdims=True)
    rows[0] = dt
    rows[1] = da
    rows[2] = acs
    rows[3] = dt * jnp.exp(tot - acs)
    rows[4] = jnp.broadcast_to(jnp.exp(tot), (nh, q))

    lane_lo = lax.broadcasted_iota(I32, (q, V7X_LANES), 1) < half
    r2 = lax.broadcasted_iota(I32, (2 * D_STATE, V7X_LANES), 0) < D_STATE
    l2 = lax.broadcasted_iota(I32, (2 * D_STATE, V7X_LANES), 1) < half
    diag = r2 == l2

    def group(g, carry):
        r0 = pl.multiple_of(g * hpg, hpg)
        dtg = rows[0, pl.ds(r0, hpg), :]
        dag = rows[1, pl.ds(r0, hpg), :]
        acg = rows[2, pl.ds(r0, hpg), :]
        wtg = rows[3, pl.ds(r0, hpg), :]
        etg = rows[4, pl.ds(r0, hpg), :]
        c0 = pl.multiple_of(g * D_STATE, D_STATE)
        cg = c_ref[:, pl.ds(c0, D_STATE)]
        bg = b_ref[:, pl.ds(c0, D_STATE)]
        cb = lax.dot_general(cg.astype(BF16), bg.astype(BF16), (((1,), (1,)), ((), ())),
                             preferred_element_type=F32)
        bgt = bg.T
        yps = []
        for k in range(hpg // 2):
            x0 = pl.multiple_of(g * (hpg * HEAD_DIM) + k * V7X_LANES, V7X_LANES)
            x2 = x_ref[:, pl.ds(x0, V7X_LANES)]
            ms, ces, bws, ess = [], [], [], []
            for t in range(2):
                j = 2 * k + t
                acol = jnp.sum(lm * dag[j:j + 1, :], axis=1, keepdims=True)
                dec = jnp.where(keep, jnp.exp(acol - acg[j:j + 1, :]), 0.0)
                ms.append(cb * dec * dtg[j:j + 1, :])
                ces.append(cg * jnp.exp(acol))
                bws.append(bgt * wtg[j:j + 1, :])
                ess.append(jnp.broadcast_to(etg[j:j + 1, :], (D_STATE, V7X_LANES)))
            sp = st[g * (hpg // 2) + k]
            lhs = jnp.concatenate(ms + ces, axis=1).astype(BF16)
            rhs = jnp.concatenate([jnp.where(lane_lo, x2, 0.0), jnp.where(lane_lo, 0.0, x2), sp],
                                  axis=0).astype(BF16)
            yp = jnp.dot(lhs, rhs, preferred_element_type=F32)
            if has_prev:
                yp = yp + yp_ref[:, pl.ds(x0, V7X_LANES)]
            if gated:
                yps.append(yp)
            else:
                y_ref[:, pl.ds(x0, V7X_LANES)] = yp
            upd = jnp.dot(jnp.concatenate(bws, axis=0).astype(BF16), x2.astype(BF16),
                          preferred_element_type=F32)
            st[g * (hpg // 2) + k] = sp * jnp.concatenate(ess, axis=0) + jnp.where(diag, upd, 0.0)
        if gated:
            lanes = pl.ds(pl.multiple_of(g * (hpg * HEAD_DIM), hpg * HEAD_DIM), hpg * HEAD_DIM)
            v = (jnp.concatenate(yps, axis=1) + x_ref[:, lanes] * dsk_ref[:, lanes]) * _silu(z_ref[:, lanes])
            r = lax.rsqrt(jnp.mean(v * v, axis=-1, keepdims=True) + EPS)
            y_ref[:, lanes] = (v * r * nw_ref[:, lanes]).astype(y_ref.dtype)
        return carry

    lax.fori_loop(0, N_GROUPS, group, 0, unroll=True)

    @pl.when(step == pl.num_programs(0) - 1)
    def _():
        sf_ref[...] = st[...]


def ssd_scan_dir(xbc, dt_t, a_row, s0, y_prev, *, direction, gate=None):
    n = xbc.shape[0]
    nh = dt_t.shape[0] // 2
    d_inner = nh * HEAD_DIM
    gn = N_GROUPS * D_STATE
    assert n % CHUNK == 0 and d_inner % gn == 0
    nc = n // CHUNK
    rev = direction == 1
    cidx = (lambda i: nc - 1 - i) if rev else (lambda i: i)
    ins = [xbc, xbc, xbc, dt_t, a_row, s0]
    specs = [
        pl.BlockSpec((CHUNK, d_inner), lambda i: (cidx(i), 0)),
        pl.BlockSpec((CHUNK, gn), lambda i: (cidx(i), d_inner // gn)),
        pl.BlockSpec((CHUNK, gn), lambda i: (cidx(i), d_inner // gn + 1)),
        pl.BlockSpec((nh, CHUNK), lambda i: (direction, cidx(i))),
        pl.BlockSpec((nh, CHUNK), lambda i: (0, 0)),
        pl.BlockSpec(s0.shape, lambda i: (0, 0, 0)),
    ]
    if y_prev is not None:
        ins.append(y_prev)
        specs.append(pl.BlockSpec((CHUNK, d_inner), lambda i: (cidx(i), 0)))
    if gate is not None:
        zx, d_row, norm_w = gate
        vec = pl.BlockSpec((1, d_inner), lambda i: (0, 0))
        ins += [zx, d_row.reshape(1, d_inner), norm_w.reshape(1, d_inner)]
        specs += [pl.BlockSpec((CHUNK, d_inner), lambda i: (cidx(i), 0)), vec, vec]
    return pl.pallas_call(
        functools.partial(_scan_kernel, rev=rev, has_prev=y_prev is not None, gated=gate is not None),
        grid=(nc,),
        in_specs=specs,
        out_specs=[pl.BlockSpec((CHUNK, d_inner), lambda i: (cidx(i), 0)),
                   pl.BlockSpec(s0.shape, lambda i: (0, 0, 0))],
        out_shape=[jax.ShapeDtypeStruct((n, d_inner), F32 if gate is None else BF16),
                   jax.ShapeDtypeStruct(s0.shape, F32)],
        scratch_shapes=[pltpu.VMEM(s0.shape, F32), pltpu.VMEM((5, nh, CHUNK), F32)],
        compiler_params=_params("arbitrary"),
        name="scan_bwd" if rev else "scan_fwd",
    )(*ins)


def _row_tile(n, pref):
    return pref if n % pref == 0 else n


def ssd_layer(x, ctx_s, mod_l, mod_c, norm1, w_in, conv_w, conv_b, dt_bias, a_log, d_skip, norm_w, w_out, k, ctx_out):
    nh = a_log.shape[1]
    d_inner = nh * HEAD_DIM
    gn = N_GROUPS * D_STATE
    a_rows = jnp.broadcast_to(a_log.astype(F32)[:, :, None], (2, nh, CHUNK))
    d_row = jnp.repeat(d_skip.astype(F32), HEAD_DIM)
    in_tn = w_in.shape[-1] // 9

    def project(v, mod, width):
        n = v.shape[0]
        zx = matmul(v, w_in, w_index=k, mod=(norm1, mod[1], mod[0]), tm=_row_tile(n, 1024), tn=in_tn,
                    name="ssd_in")
        xbc = conv_silu(zx, conv_w, conv_b, width=width, col0=d_inner, tb=_row_tile(n, 1024))
        dt_t = dt_rows(zx, dt_bias, col0=2 * d_inner + 2 * gn, tb=_row_tile(n, 512))
        return zx, xbc, dt_t

    zx_c, xbc_c, dt_c = project(ctx_s, mod_c, ctx_s.shape[0])
    zx_l, xbc_l, dt_l = project(x, mod_l, GRID_W)
    zero = jnp.zeros((nh // 2, 2 * D_STATE, V7X_LANES), F32)
    yc, s_f = ssd_scan_dir(xbc_c, dt_c, a_rows[0], zero, None, direction=0)
    yl, _ = ssd_scan_dir(xbc_l, dt_l, a_rows[0], s_f, None, direction=0)
    yc, s_b = ssd_scan_dir(xbc_c, dt_c, a_rows[1], zero, yc, direction=1,
                           gate=(zx_c, d_row, norm_w) if ctx_out else None)
    yl, _ = ssd_scan_dir(xbc_l, dt_l, a_rows[1], s_b, yl, direction=1, gate=(zx_l, d_row, norm_w))

    def output(v, yn, gate):
        return matmul(yn, w_out, w_index=k, res=v, gate=gate, tm=_row_tile(v.shape[0], 512), tn=1024,
                      name="ssd_out")

    x_new = output(x, yl, mod_l[2])
    ctx_new = output(ctx_s, yc, mod_c[2]) if ctx_out else None
    return x_new, ctx_new


def _dft_tables(n):
    j = jnp.arange(n, dtype=I32)
    ang = ((j[:, None] * j[None, :]) % n).astype(F32) * (2.0 * math.pi / n)
    return jnp.cos(ang), jnp.sin(ang)


def _fchan_kernel(x_ref, gain_ref, sc_ref, sh_ref, cs_ref, o_ref):
    h = _rms_mod(x_ref[...], gain_ref[...], sc_ref[...], sh_ref[...]).astype(BF16)
    gw = cs_ref.shape[0]
    d = h.shape[1]
    for g in range(d // gw):
        r = jnp.dot(h[:, g * gw:(g + 1) * gw], cs_ref[...], preferred_element_type=F32)
        o_ref[:, g * gw:(g + 1) * gw] = r[:, :gw]
        o_ref[:, d + g * gw:d + (g + 1) * gw] = r[:, gw:]


def _f1_kernel(r_ref, i_ref, k1_ref, tc_ref, ts_ref, yr_ref, yi_ref):
    la, sub, cw = r_ref.shape
    rows = la * sub
    xx = jnp.concatenate([r_ref[...].reshape(rows, cw), i_ref[...].reshape(rows, cw)], axis=0).astype(BF16)
    y = jnp.dot(k1_ref[...], xx, preferred_element_type=F32)
    yr, yi = y[:rows], y[rows:]
    c, s = tc_ref[...], ts_ref[...]
    yr_ref[...] = (c * yr + s * yi).reshape(la, sub, cw)
    yi_ref[...] = (c * yi - s * yr).reshape(la, sub, cw)


def _f2_kernel(r_ref, i_ref, k2_ref, o_ref):
    sub, lb, cw = r_ref.shape
    xx = jnp.concatenate([r_ref[...].reshape(sub * lb, cw), i_ref[...].reshape(sub * lb, cw)],
                         axis=0).astype(BF16)
    o_ref[...] = jnp.dot(k2_ref[...], xx, preferred_element_type=F32).reshape(lb, sub, cw)


def fnet_layer(x, mod, norm1, w_out, k, *, cw=512):
    l, d = x.shape
    sub = V7X_SUBLANES
    la = l // 128 if l >= 1024 else sub
    lb = l // la
    gw = d // FNET_GROUPS
    assert la % sub == 0 and lb % sub == 0 and d % cw == 0
    shift, scale, gate = mod
    cc, sc = _dft_tables(gw)
    norm = 1.0 / math.sqrt(l * gw)
    cs = (jnp.concatenate([cc, -sc], axis=1) * norm).astype(BF16)
    tm = _row_tile(l, 512)
    vec = pl.BlockSpec((1, d), lambda i: (0, 0))
    g2 = pl.pallas_call(
        _fchan_kernel,
        grid=(l // tm,),
        in_specs=[pl.BlockSpec((tm, d), lambda i: (i, 0)), vec, vec, vec, pl.BlockSpec((gw, 2 * gw), lambda i: (0, 0))],
        out_specs=pl.BlockSpec((tm, 2 * d), lambda i: (i, 0)),
        out_shape=jax.ShapeDtypeStruct((l, 2 * d), F32),
        compiler_params=_params("arbitrary"),
        name="fnet_chan",
    )(x, norm1.reshape(1, d), scale.reshape(1, d), shift.reshape(1, d), cs)
    ca, sa = _dft_tables(la)
    eye = jnp.eye(sub, dtype=F32)
    k1 = jnp.concatenate([jnp.concatenate([jnp.kron(ca, eye), jnp.kron(sa, eye)], axis=1),
                          jnp.concatenate([jnp.kron(-sa, eye), jnp.kron(ca, eye)], axis=1)], axis=0).astype(BF16)
    bi = jnp.arange(lb, dtype=I32).reshape(lb // sub, 1, sub)
    di = jnp.arange(la, dtype=I32).reshape(1, la, 1)
    ang = ((bi * di) % l).astype(F32) * (2.0 * math.pi / l)
    tc = jnp.cos(ang).reshape(lb // sub, la * sub, 1)
    ts = jnp.sin(ang).reshape(lb // sub, la * sub, 1)
    g3 = g2.reshape(la, lb, 2 * d)
    ncb = d // cw
    blk1 = lambda off: pl.BlockSpec((la, sub, cw), lambda b, c: (0, b, off + c))
    twb = pl.BlockSpec((None, la * sub, 1), lambda b, c: (b, 0, 0))
    yr3, yi3 = pl.pallas_call(
        _f1_kernel,
        grid=(lb // sub, ncb),
        in_specs=[blk1(0), blk1(ncb), pl.BlockSpec(k1.shape, lambda b, c: (0, 0)), twb, twb],
        out_specs=[blk1(0), blk1(0)],
        out_shape=[jax.ShapeDtypeStruct((la, lb, d), F32)] * 2,
        compiler_params=_params("arbitrary", "arbitrary"),
        name="fnet_stage1",
    )(g3, g3, k1, tc, ts)
    cb_, sb_ = _dft_tables(lb)
    k2 = jnp.concatenate([jnp.einsum("cb,de->cdeb", cb_, eye).reshape(lb * sub, sub * lb),
                          jnp.einsum("cb,de->cdeb", sb_, eye).reshape(lb * sub, sub * lb)], axis=1).astype(BF16)
    blk2 = pl.BlockSpec((sub, lb, cw), lambda dd, c: (dd, 0, c))
    f3 = pl.pallas_call(
        _f2_kernel,
        grid=(la // sub, ncb),
        in_specs=[blk2, blk2, pl.BlockSpec(k2.shape, lambda dd, c: (0, 0))],
        out_specs=pl.BlockSpec((lb, sub, cw), lambda dd, c: (0, dd, c)),
        out_shape=jax.ShapeDtypeStruct((lb, la, d), F32),
        compiler_params=_params("arbitrary", "arbitrary"),
        name="fnet_stage2",
    )(yr3, yi3, k2)
    return matmul(f3.reshape(l, d), w_out, w_index=k, res=x, gate=gate, tm=_row_tile(l, 512), tn=1024,
                  name="fnet_out")


ROUTE_S = 128
ROUTE_TOKENS = ROUTE_S * ROUTE_S


def _moe_prep_kernel(x_ref, gain_ref, sc_ref, sh_ref, wr_ref, h_ref, lg_ref):
    h = _rms_mod(x_ref[...], gain_ref[...], sc_ref[...], sh_ref[...])
    tb = h.shape[0]
    for j in range(TOK_ROWS):
        h_ref[pl.ds(j, tb, stride=TOK_ROWS), :] = h[:, j * V7X_LANES:(j + 1) * V7X_LANES]
    lg_ref[...] = lax.dot_general(wr_ref[...], h, (((1,), (1,)), ((), ())), precision=HIGHEST,
                                  preferred_element_type=F32)


def moe_prep(x, norm2, scale, shift, w_router, *, tb):
    n, d = x.shape
    e = w_router.shape[1]
    assert d == TOK_ROWS * V7X_LANES and n % tb == 0
    vec = pl.BlockSpec((1, d), lambda i: (0, 0))
    return pl.pallas_call(
        _moe_prep_kernel,
        grid=(n // tb,),
        in_specs=[pl.BlockSpec((tb, d), lambda i: (i, 0)), vec, vec, vec, pl.BlockSpec((e, d), lambda i: (0, 0))],
        out_specs=[pl.BlockSpec((tb * TOK_ROWS, V7X_LANES), lambda i: (i, 0)), pl.BlockSpec((e, tb), lambda i: (0, i))],
        out_shape=[jax.ShapeDtypeStruct((n * TOK_ROWS, V7X_LANES), F32), jax.ShapeDtypeStruct((e, n), F32)],
        compiler_params=_params("arbitrary"),
        name="moe_prep",
    )(x, norm2.reshape(1, d), scale.reshape(1, d), shift.reshape(1, d), w_router.T)


def _tok_cumsum(m, lincl_bf16, ustrict):
    wc = jnp.dot(lincl_bf16, m.astype(BF16), preferred_element_type=F32)
    coltot = wc[ROUTE_S - 1:ROUTE_S, :]
    colpref = jnp.dot(jnp.broadcast_to(coltot, (V7X_SUBLANES, ROUTE_S)), ustrict, precision=HIGHEST,
                      preferred_element_type=F32)[0:1]
    return wc, coltot, colpref


def _route_kernel(lg_ref, idx_ref, q_ref, g_ref, off_ref, cnt_ref, aff_scr, bits_scr, sel_scr, qt_scr, *,
                  n_valid, cap, pc):
    ne = lg_ref.shape[0]
    s_ = ROUTE_S
    c_pad = idx_ref.shape[2]
    si = lax.broadcasted_iota(I32, (s_, s_), 0)
    ji = lax.broadcasted_iota(I32, (s_, s_), 1)
    valid = ji * s_ + si < n_valid
    lincl = (ji <= si).astype(BF16)
    ustrict = (si < ji).astype(F32)
    ones = jnp.ones((s_, s_), BF16)

    ls = [lg_ref[e] for e in range(ne)]
    mx = functools.reduce(jnp.maximum, ls)
    ex = [jnp.exp(l - mx) for l in ls]
    den = functools.reduce(jnp.add, ex)
    for e in range(ne):
        aff = jnp.where(valid, ex[e] / den, -1.0)
        aff_scr[e] = aff
        bits_scr[e] = pltpu.bitcast(aff, I32)

    def bit_step(i, ts):
        bit = jnp.left_shift(jnp.int32(1), 30 - i)
        out = []
        for e in range(ne):
            cand = ts[e] | bit
            ge = (bits_scr[e] >= cand).astype(F32)
            c1 = jnp.sum(jnp.sum(ge, axis=0, keepdims=True), axis=1, keepdims=True)
            out.append(jnp.where(c1 >= cap, cand, ts[e]))
        return tuple(out)

    ts = lax.fori_loop(0, 31, bit_step, tuple(jnp.zeros((1, 1), I32) for _ in range(ne)))

    cnt = jnp.zeros((s_, s_), F32)
    for e in range(ne):
        bits = bits_scr[e]
        gt = bits > ts[e]
        eq = (bits == ts[e]).astype(F32)
        n_gt = jnp.sum(jnp.sum(gt.astype(F32), axis=0, keepdims=True), axis=1, keepdims=True)
        wc, _, colpref = _tok_cumsum(eq, lincl, ustrict)
        rank_eq = wc + colpref - eq
        sel = jnp.where(gt | ((eq > 0.0) & (rank_eq < cap - n_gt)), 1.0, 0.0)
        sel_scr[e] = sel
        qt_scr[e] = cnt
        cnt = cnt + sel
    wc, _, colpref = _tok_cumsum(cnt, lincl, ustrict)
    offs = wc + colpref - cnt
    off_ref[...] = offs
    cnt_ref[...] = cnt

    def per_expert(e, carry):
        m = sel_scr[e]
        qt = qt_scr[e] + offs
        aff = aff_scr[e]
        wc, coltot, colpref = _tok_cumsum(m, lincl, ustrict)
        wc_b = wc.astype(BF16)
        colcum = jnp.dot((lincl.astype(F32) * coltot).astype(BF16), ones, preferred_element_type=F32)
        colpref8 = jnp.broadcast_to(colpref, (V7X_SUBLANES, s_))
        sub = lax.broadcasted_iota(I32, (s_, pc), 0).astype(F32)
        reps = pc // s_
        colcum_t = jnp.concatenate([colcum] * reps, axis=1) if reps > 1 else colcum
        for c0 in range(0, c_pad, pc):
            p = (lax.broadcasted_iota(I32, (s_, pc), 1) + c0).astype(F32)
            p_row = p[0:1]
            blk = jnp.sum((colcum_t <= p).astype(F32), axis=0, keepdims=True)
            oh_j = (sub == blk).astype(F32)
            colvec = jnp.dot(wc_b, oh_j.astype(BF16), preferred_element_type=F32)
            cp = jnp.dot(colpref8, oh_j, precision=HIGHEST, preferred_element_type=F32)[0:1]
            s_idx = jnp.sum((colvec <= p_row - cp).astype(F32), axis=0, keepdims=True)
            oh_s = (sub == s_idx).astype(F32)
            gv = jnp.sum(oh_s * jnp.dot(aff, oh_j, precision=HIGHEST, preferred_element_type=F32),
                         axis=0, keepdims=True)
            qv = jnp.sum(oh_s * jnp.dot(qt, oh_j, precision=HIGHEST, preferred_element_type=F32),
                         axis=0, keepdims=True)
            live = p_row < cap
            idx_ref[e, :, c0:c0 + pc] = jnp.where(live, blk * s_ + s_idx, 0.0).astype(I32)
            q_ref[e, :, c0:c0 + pc] = jnp.where(live, qv, 0.0).astype(I32)
            g_ref[e, :, c0:c0 + pc] = jnp.where(live, gv, 0.0)
        return carry

    lax.fori_loop(0, ne, per_expert, 0)


def moe_route(logits_t, *, cap):
    ne, n = logits_t.shape
    assert n <= ROUTE_TOKENS and cap <= n
    s_ = ROUTE_S
    c_pad = max(cap, s_)
    pc = min(c_pad, 2 * s_)
    assert c_pad % pc == 0
    lg = jnp.pad(logits_t, ((0, 0), (0, ROUTE_TOKENS - n))).reshape(ne, s_, s_).transpose(0, 2, 1)
    full = lambda shape: pl.BlockSpec(shape, lambda i: (0,) * len(shape))
    slot = jax.ShapeDtypeStruct((ne, 1, c_pad), I32)
    idx, q, g, offs, cnt = pl.pallas_call(
        functools.partial(_route_kernel, n_valid=n, cap=cap, pc=pc),
        grid=(1,),
        in_specs=[full((ne, s_, s_))],
        out_specs=[full((ne, 1, c_pad))] * 3 + [full((s_, s_))] * 2,
        out_shape=[slot, slot, jax.ShapeDtypeStruct((ne, 1, c_pad), F32),
                   jax.ShapeDtypeStruct((s_, s_), F32), jax.ShapeDtypeStruct((s_, s_), F32)],
        scratch_shapes=[pltpu.VMEM((ne, s_, s_), F32), pltpu.VMEM((ne, s_, s_), I32),
                        pltpu.VMEM((ne, s_, s_), F32), pltpu.VMEM((ne, s_, s_), F32)],
        compiler_params=_params("arbitrary"),
        name="moe_route",
    )(lg)
    tok = lambda a: a.T.reshape(-1)[:n]
    return idx[:, 0, :cap], q[:, 0, :cap], g[:, 0, :cap], tok(offs), tok(cnt)


def _ffn_kernel(idx_ref, q_ref, g_ref, h_hbm, wg_ref, wu_ref, wd_ref, z_hbm, ga, gb, sa, sb, sems):
    step = pl.program_id(0)
    last = pl.num_programs(0) - 1
    cbh = ga.shape[0] // TOK_ROWS
    total = idx_ref.shape[0]
    base = step * (2 * cbh)
    srows = lambda p: pl.ds(p * TOK_ROWS, TOK_ROWS)
    drows = lambda r: pl.ds(pl.multiple_of(r * TOK_ROWS, TOK_ROWS), TOK_ROWS)
    G_A, G_B, S_A, S_B = range(4)

    def gather_start(buf, sem, slot0):
        for p in range(cbh):
            pltpu.make_async_copy(h_hbm.at[drows(idx_ref[slot0 + p]), :], buf.at[srows(p), :],
                                  sems.at[sem]).start(priority=p % 2)

    def gather_wait(buf, sem):
        for p in range(cbh):
            pltpu.make_async_copy(h_hbm.at[srows(0), :], buf.at[srows(p), :], sems.at[sem]).wait()

    def scatter_start(buf, sem, slot0):
        for p in range(cbh):
            pltpu.make_async_copy(buf.at[srows(p), :], z_hbm.at[drows(q_ref[slot0 + p]), :],
                                  sems.at[sem]).start(priority=p % 2)

    def scatter_wait(buf, sem):
        for p in range(cbh):
            pltpu.make_async_copy(buf.at[srows(p), :], z_hbm.at[srows(0), :], sems.at[sem]).wait()

    def compute(gbuf, sbuf, g):
        x = jnp.concatenate([gbuf[pl.ds(j, cbh, stride=TOK_ROWS), :] for j in range(TOK_ROWS)],
                            axis=1).astype(BF16)
        a = jnp.dot(x, wg_ref[...], preferred_element_type=F32)
        u = jnp.dot(x, wu_ref[...], preferred_element_type=F32)
        y = jnp.dot((_silu(a) * u).astype(BF16), wd_ref[...], preferred_element_type=F32) * g
        for j in range(TOK_ROWS):
            sbuf[pl.ds(j, cbh, stride=TOK_ROWS), :] = y[:, j * V7X_LANES:(j + 1) * V7X_LANES]

    @pl.when(step == 0)
    def _():
        gather_start(ga, G_A, 0)

    gather_wait(ga, G_A)

    @pl.when(step > 0)
    def _():
        scatter_wait(sa, S_A)

    gather_start(gb, G_B, base + cbh)
    compute(ga, sa, g_ref[0:cbh, :])
    scatter_start(sa, S_A, base)
    gather_wait(gb, G_B)

    @pl.when(step > 0)
    def _():
        scatter_wait(sb, S_B)

    gather_start(ga, G_A, jnp.minimum(base + 2 * cbh, total - cbh))
    compute(gb, sb, g_ref[cbh:2 * cbh, :])
    scatter_start(sb, S_B, base + cbh)

    @pl.when(step == last)
    def _():
        gather_wait(ga, G_A)
        scatter_wait(sa, S_A)
        scatter_wait(sb, S_B)


def moe_ffn(h_rows, idx, q, g, w_gate, w_up, w_down, layer, *, cbh):
    ne, cap = idx.shape
    d, f = w_gate.shape[2:]
    sb_rows = 2 * cbh
    assert cap % sb_rows == 0
    spe = cap // sb_rows
    wmap = lambda s, *_: (layer, s // spe, 0, 0)
    buf = pltpu.VMEM((cbh * TOK_ROWS, V7X_LANES), F32)
    return pl.pallas_call(
        _ffn_kernel,
        grid_spec=pltpu.PrefetchScalarGridSpec(
            num_scalar_prefetch=2,
            grid=(ne * spe,),
            in_specs=[
                pl.BlockSpec((sb_rows, 1), lambda s, *_: (s, 0)),
                pl.BlockSpec(memory_space=pl.ANY),
                pl.BlockSpec((None, None, d, f), wmap),
                pl.BlockSpec((None, None, d, f), wmap),
                pl.BlockSpec((None, None, f, d), wmap),
            ],
            out_specs=pl.BlockSpec(memory_space=pl.ANY),
            scratch_shapes=[buf, buf, buf, buf, pltpu.SemaphoreType.DMA((4,))],
        ),
        out_shape=jax.ShapeDtypeStruct((ne * cap * TOK_ROWS, V7X_LANES), F32),
        compiler_params=_params("arbitrary"),
        name="moe_ffn",
    )(idx.reshape(-1), q.reshape(-1), g.reshape(ne * cap, 1), h_rows, w_gate, w_up, w_down)


def _combine_kernel(boff_ref, x_ref, gate_ref, off_ref, cnt_ref, z_hbm, *rest, ch, p_total, final):
    fw_ref = rest[0] if final else None
    o_ref, z0, z1, sems = rest[1:] if final else rest
    i = pl.program_id(0)
    start = boff_ref[i]
    end = boff_ref[i + 1]
    nch = (end - start + ch - 1) // ch
    lo = off_ref[...]
    hi = lo + cnt_ref[...]
    o_ref[...] = jnp.zeros(o_ref.shape, F32)
    zbufs = (z0, z1)

    def row_start(c, first_row=None):
        first_row = start if first_row is None else first_row
        return jnp.minimum(first_row + c * ch, p_total - ch)

    def copy(c, slot, first_row=None):
        src = z_hbm.at[pl.ds(pl.multiple_of(row_start(c, first_row) * TOK_ROWS, TOK_ROWS), ch * TOK_ROWS), :]
        return pltpu.make_async_copy(src, zbufs[slot], sems.at[slot])

    def process(c, slot):
        copy(c, slot).wait()

        @pl.when(c + 1 < nch)
        def _():
            copy(c + 1, 1 - slot).start()

        first = start + c * ch
        qabs = row_start(c) + lax.broadcasted_iota(I32, (1, ch), 1)
        qf = qabs.astype(F32)
        seg = ((qf >= lo) & (qf < hi) & (qabs >= first)).astype(BF16)
        zb = zbufs[slot]
        z = jnp.concatenate([zb[pl.ds(j, ch, stride=TOK_ROWS), :] for j in range(TOK_ROWS)], axis=1)
        zh = z.astype(BF16)
        zl = (z - zh.astype(F32)).astype(BF16)
        o_ref[...] += (jnp.dot(seg, zh, preferred_element_type=F32) + jnp.dot(seg, zl, preferred_element_type=F32))

    @pl.when((i == 0) & (nch > 0))
    def _():
        copy(0, 0).start()

    def pair(k, carry):
        process(2 * k, 0)

        @pl.when(2 * k + 1 < nch)
        def _():
            process(2 * k + 1, 1)

        return carry

    lax.fori_loop(0, (nch + 1) // 2, pair, 0)

    @pl.when(i + 1 < pl.num_programs(0))
    def _():
        @pl.when(boff_ref[i + 2] > end)
        def _():
            copy(0, 0, first_row=end).start()

    out = x_ref[...] + gate_ref[...] * o_ref[...]
    if final:
        out = out * lax.rsqrt(jnp.mean(out * out, axis=-1, keepdims=True) + EPS) * fw_ref[...]
    o_ref[...] = out


def moe_combine(x, gate, z_rows, offs, cnt, *, tb, ch=256, final_w=None):
    n, d = x.shape
    p_total = z_rows.shape[0] // TOK_ROWS
    assert n % tb == 0 and p_total >= ch
    boff = jnp.concatenate([offs[::tb], jnp.full((1,), p_total, F32)]).astype(I32)
    col = pl.BlockSpec((tb, 1), lambda i, s: (i, 0))
    vec = pl.BlockSpec((1, d), lambda i, s: (0, 0))
    extra_in, extra_spec = ([final_w.reshape(1, d)], [vec]) if final_w is not None else ([], [])
    return pl.pallas_call(
        functools.partial(_combine_kernel, ch=ch, p_total=p_total, final=final_w is not None),
        grid_spec=pltpu.PrefetchScalarGridSpec(
            num_scalar_prefetch=1,
            grid=(n // tb,),
            in_specs=[pl.BlockSpec((tb, d), lambda i, s: (i, 0)), vec,
                      col, col, pl.BlockSpec(memory_space=pl.ANY)] + extra_spec,
            out_specs=pl.BlockSpec((tb, d), lambda i, s: (i, 0)),
            scratch_shapes=[pltpu.VMEM((ch * TOK_ROWS, V7X_LANES), F32), pltpu.VMEM((ch * TOK_ROWS, V7X_LANES), F32),
                            pltpu.SemaphoreType.DMA((2,))],
        ),
        out_shape=jax.ShapeDtypeStruct((n, d), F32),
        compiler_params=_params("arbitrary"),
        name="moe_combine",
    )(boff, x, gate.reshape(1, d), offs.reshape(n, 1), cnt.reshape(n, 1), z_rows, *extra_in)


def moe_layer(x, mod, norm2, w_router, w_gate, w_up, w_down, layer, final_w=None):
    n = x.shape[0]
    ne = w_router.shape[1]
    cap = (CAPACITY_FACTOR * n) // ne
    shift, scale, gate = mod
    h_rows, logits_t = moe_prep(x, norm2, scale, shift, w_router, tb=_row_tile(n, 256))
    idx, q, g, offs, cnt = moe_route(logits_t, cap=cap)
    z_rows = moe_ffn(h_rows, idx, q, g, w_gate, w_up, w_down, layer, cbh=min(cap // 2, 256))
    return moe_combine(x, gate, z_rows, offs, cnt, tb=_row_tile(n, 512), final_w=final_w)


def kernel(x, c, ctx, c_ctx, ada_w, ada_b, norm1_w, norm2_w, final_norm_w, ssd_w_in, ssd_conv_w, ssd_conv_b,
           ssd_dt_bias, ssd_a_log, ssd_d, ssd_norm_w, ssd_w_out, fnet_w_out, moe_w_router, moe_w_gate, moe_w_up,
           moe_w_down):
    depth = ada_w.shape[0]
    d = x.shape[-1]
    assert x.shape[0] == 1, "one sample per call"
    xs, cs = x[0], ctx[0]
    ada = ada_all(c, c_ctx, ada_w, ada_b)
    moe_w = (moe_w_gate.astype(BF16), moe_w_up.astype(BF16), moe_w_down.astype(BF16))
    w_in_b, w_out_b, w_fnet_b = ssd_w_in.astype(BF16), ssd_w_out.astype(BF16), fnet_w_out.astype(BF16)
    for i in range(depth):
        k = i // 2
        is_ssd = i % 2 == 0
        ctx_later = any(j % 2 == 0 for j in range(i + 1, depth))
        sh1, sc1, g1, sh2, sc2, g2 = (ada[i, 0, m * d:(m + 1) * d] for m in range(6))
        csh1, csc1, cg1, csh2, csc2, cg2 = (ada[i, 1, m * d:(m + 1) * d] for m in range(6))
        if is_ssd:
            xs, cs_new = ssd_layer(xs, cs, (sh1, sc1, g1), (csh1, csc1, cg1), norm1_w[i], w_in_b, ssd_conv_w[k],
                                   ssd_conv_b[k], ssd_dt_bias[k], ssd_a_log[k], ssd_d[k], ssd_norm_w[k],
                                   w_out_b, k, ctx_later)
        else:
            xs = fnet_layer(xs, (sh1, sc1, g1), norm1_w[i], w_fnet_b, k)
            cs_new = fnet_layer(cs, (csh1, csc1, cg1), norm1_w[i], w_fnet_b, k) if ctx_later else None
        xs = moe_layer(xs, (sh2, sc2, g2), norm2_w[i], moe_w_router[i], *moe_w, i,
                       final_w=final_norm_w if i == depth - 1 else None)
        if ctx_later:
            cs = moe_layer(cs_new, (csh2, csc2, cg2), norm2_w[i], moe_w_router[i], *moe_w, i)
    return xs[None]
```

```python
import functools
import math

import jax
import jax.numpy as jnp
from jax import lax
from jax.experimental import pallas as pl
from jax.experimental.pallas import tpu as pltpu

F32 = jnp.float32
BF16 = jnp.bfloat16
I32 = jnp.int32
HIGHEST = lax.Precision.HIGHEST

V7X_LANES = 128
V7X_SUBLANES = 8
V7X_VMEM_BYTES = 64 * 1024 * 1024
VMEM_LIMIT = V7X_VMEM_BYTES - 8 * 1024 * 1024

GRID_W = 64
HEAD_DIM = 64
N_GROUPS = 8
D_STATE = 128
CHUNK = 128
FNET_GROUPS = 8
N_EXPERTS = 16
CAPACITY_FACTOR = 2
EPS = 1e-6
TOK_ROWS = 16


def _params(*sem):
    return pltpu.CompilerParams(dimension_semantics=sem, vmem_limit_bytes=VMEM_LIMIT)


def _rms_mod(x, gain, scale, shift):
    y = x * lax.rsqrt(jnp.mean(x * x, axis=-1, keepdims=True) + EPS)
    return y * gain * (1.0 + scale) + shift


def _silu(x):
    return x * jax.nn.sigmoid(x)


def _ada_kernel(c_ref, w_ref, b_ref, o_ref):
    w = w_ref[...]
    rows = []
    for r in range(2):
        s = _silu(c_ref[r])
        rows.append(jnp.sum(s * w, axis=0, keepdims=True) + b_ref[...])
    rows.append(jnp.zeros((V7X_SUBLANES - 2, w.shape[1]), F32))
    o_ref[...] = jnp.concatenate(rows, axis=0)


def ada_all(c, c_ctx, ada_w, ada_b):
    depth, k, n6 = ada_w.shape
    tn = n6 // 8
    cond = jnp.stack([c[0], c_ctx], axis=0)[:, :, None]
    out = pl.pallas_call(
        _ada_kernel,
        grid=(depth, n6 // tn),
        in_specs=[
            pl.BlockSpec((2, k, 1), lambda l, n: (0, 0, 0)),
            pl.BlockSpec((None, k, tn), lambda l, n: (l, 0, n)),
            pl.BlockSpec((None, 1, tn), lambda l, n: (l, 0, n)),
        ],
        out_specs=pl.BlockSpec((None, V7X_SUBLANES, tn), lambda l, n: (l, 0, n)),
        out_shape=jax.ShapeDtypeStruct((depth, V7X_SUBLANES, n6), F32),
        compiler_params=_params("arbitrary", "arbitrary"),
        name="ada",
    )(cond, ada_w, ada_b[:, None, :])
    return out


def _mm_kernel(*refs, prologue, epilogue):
    it = iter(refs)
    a_ref = next(it)
    if prologue:
        gain_ref, sc_ref, sh_ref = next(it), next(it), next(it)
    w_ref = next(it)
    if epilogue:
        res_ref, gate_ref = next(it), next(it)
    o_ref = next(it)
    a_scr = next(it, None)

    if a_scr is None:
        a_bf16 = a_ref[...]
    else:
        @pl.when(pl.program_id(1) == 0)
        def _():
            a = a_ref[...]
            if prologue:
                a = _rms_mod(a, gain_ref[...], sc_ref[...], sh_ref[...])
            a_scr[...] = a.astype(BF16)

        a_bf16 = a_scr[...]
    acc = jnp.dot(a_bf16, w_ref[...], preferred_element_type=F32)
    if epilogue:
        acc = res_ref[...] + gate_ref[...] * acc
    o_ref[...] = acc.astype(o_ref.dtype)


def matmul(a, w_bf16, *, w_index=None, mod=None, res=None, gate=None, tm, tn, out_dtype=F32, name="mm",
           w_outer=False):
    m, k = a.shape
    n = w_bf16.shape[-1]
    assert m % tm == 0 and n % tn == 0
    use_scr = mod is not None or a.dtype != BF16
    assert not (w_outer and use_scr)
    ij = (lambda g0, g1: (g1, g0)) if w_outer else (lambda g0, g1: (g0, g1))
    row = lambda g0, g1: (0, 0)
    ins, specs = [a], [pl.BlockSpec((tm, k), lambda g0, g1: (ij(g0, g1)[0], 0))]
    if mod is not None:
        for v in mod:
            ins.append(v.reshape(1, k))
            specs.append(pl.BlockSpec((1, k), row))
    ins.append(w_bf16)
    if w_index is None:
        specs.append(pl.BlockSpec((k, tn), lambda g0, g1: (0, ij(g0, g1)[1])))
    else:
        specs.append(pl.BlockSpec((None, k, tn), lambda g0, g1: (w_index, 0, ij(g0, g1)[1])))
    if res is not None:
        ins += [res, gate.reshape(1, n)]
        specs += [pl.BlockSpec((tm, tn), ij), pl.BlockSpec((1, tn), lambda g0, g1: (0, ij(g0, g1)[1]))]
    return pl.pallas_call(
        functools.partial(_mm_kernel, prologue=mod is not None, epilogue=res is not None),
        grid=(n // tn, m // tm) if w_outer else (m // tm, n // tn),
        in_specs=specs,
        out_specs=pl.BlockSpec((tm, tn), ij),
        out_shape=jax.ShapeDtypeStruct((m, n), out_dtype),
        scratch_shapes=[pltpu.VMEM((tm, k), BF16)] if use_scr else [],
        compiler_params=_params("arbitrary", "arbitrary"),
        name=name,
    )(*ins)


def _conv_kernel(prev_ref, cur_ref, next_ref, w_ref, b_ref, o_ref, *, width):
    i = pl.program_id(0)
    last = pl.num_programs(0) - 1
    cur = cur_ref[...]
    tb, cb = cur.shape
    prev = jnp.where(i > 0, prev_ref[...], 0.0)
    nxt = jnp.where(i < last, next_ref[...], 0.0)
    if tb > width:
        up = jnp.concatenate([prev, cur[: tb - width]], axis=0)
        dn = jnp.concatenate([cur[width:], nxt], axis=0)
    else:
        up, dn = prev, nxt
    w = w_ref[...]

    def tap(dw):
        return up * w[dw:dw + 1] + cur * w[3 + dw:4 + dw] + dn * w[6 + dw:7 + dw]

    col = lax.broadcasted_iota(I32, (tb, cb), 0) & (width - 1)
    left = jnp.where(col == 0, 0.0, pltpu.roll(tap(0), 1, axis=0))
    right = jnp.where(col == width - 1, 0.0, pltpu.roll(tap(2), tb - 1, axis=0))
    o_ref[...] = _silu(left + tap(1) + right + b_ref[...])


def conv_silu(zx, conv_w, conv_b, *, width, col0, tb, cb=512):
    n = zx.shape[0]
    c = conv_w.shape[-1]
    assert width & (width - 1) == 0 and tb % width == 0 and n % tb == 0 and c % cb == 0 and col0 % cb == 0
    cblk0 = col0 // cb
    per = tb // width
    nrow = n // width
    return pl.pallas_call(
        functools.partial(_conv_kernel, width=width),
        grid=(n // tb, c // cb),
        in_specs=[
            pl.BlockSpec((width, cb), lambda i, j: (jnp.maximum(i * per - 1, 0), cblk0 + j)),
            pl.BlockSpec((tb, cb), lambda i, j: (i, cblk0 + j)),
            pl.BlockSpec((width, cb), lambda i, j: (jnp.minimum((i + 1) * per, nrow - 1), cblk0 + j)),
            pl.BlockSpec((9, cb), lambda i, j: (0, j)),
            pl.BlockSpec((1, cb), lambda i, j: (0, j)),
        ],
        out_specs=pl.BlockSpec((tb, cb), lambda i, j: (i, j)),
        out_shape=jax.ShapeDtypeStruct((n, c), F32),
        compiler_params=_params("arbitrary", "arbitrary"),
        name="conv",
    )(zx, zx, zx, conv_w.reshape(9, c), conv_b.reshape(1, c))


def _dt_kernel(x_ref, b_ref, o_ref):
    v = x_ref[...] + b_ref[...]
    sp = jnp.maximum(v, 0.0) + jnp.log1p(jnp.exp(-jnp.abs(v)))
    o_ref[...] = sp.T


def dt_rows(zx, dt_bias, *, col0, tb):
    n = zx.shape[0]
    w = dt_bias.size
    assert col0 % w == 0 and n % tb == 0
    return pl.pallas_call(
        _dt_kernel,
        grid=(n // tb,),
        in_specs=[pl.BlockSpec((tb, w), lambda i: (i, col0 // w)), pl.BlockSpec((1, w), lambda i: (0, 0))],
        out_specs=pl.BlockSpec((w, tb), lambda i: (0, i)),
        out_shape=jax.ShapeDtypeStruct((w, n), F32),
        compiler_params=_params("arbitrary"),
        name="dt",
    )(zx, dt_bias.reshape(1, w))


def _scan_kernel(x_ref, b_ref, c_ref, dt_ref, a_ref, s0_ref, *rest, rev, has_prev, gated):
    rest = list(rest)
    yp_ref = rest.pop(0) if has_prev else None
    z_ref, dsk_ref, nw_ref = (rest.pop(0), rest.pop(0), rest.pop(0)) if gated else (None, None, None)
    y_ref, sf_ref, st, rows = rest
    step = pl.program_id(0)
    q = CHUNK
    nh = dt_ref.shape[0]
    hpg = nh // N_GROUPS
    half = V7X_LANES // 2

    @pl.when(step == 0)
    def _():
        st[...] = s0_ref[...]

    ri = lax.broadcasted_iota(I32, (q, q), 0)
    ci = lax.broadcasted_iota(I32, (q, q), 1)
    if rev:
        upper, keep = ri >= ci, ri <= ci
    else:
        upper, keep = ri <= ci, ri >= ci
    lm = keep.astype(F32)
    dt = dt_ref[...]
    da = dt * -jnp.exp(a_ref[...])
    acs = jnp.dot(da, upper.astype(F32), precision=HIGHEST, preferred_element_type=F32)
    tot = jnp.sum(da, axis=1, keepdims=True)
    rows[0] = dt
    rows[1] = da
    rows[2] = acs
    rows[3] = dt * jnp.exp(tot - acs)
    rows[4] = jnp.broadcast_to(jnp.exp(tot), (nh, q))

    lane_lo = lax.broadcasted_iota(I32, (q, V7X_LANES), 1) < half
    r2 = lax.broadcasted_iota(I32, (2 * D_STATE, V7X_LANES), 0) < D_STATE
    l2 = lax.broadcasted_iota(I32, (2 * D_STATE, V7X_LANES), 1) < half
    diag = r2 == l2

    def group(g, carry):
        r0 = pl.multiple_of(g * hpg, hpg)
        dtg = rows[0, pl.ds(r0, hpg), :]
        dag = rows[1, pl.ds(r0, hpg), :]
        acg = rows[2, pl.ds(r0, hpg), :]
        wtg = rows[3, pl.ds(r0, hpg), :]
        etg = rows[4, pl.ds(r0, hpg), :]
        c0 = pl.multiple_of(g * D_STATE, D_STATE)
        cg = c_ref[:, pl.ds(c0, D_STATE)]
        bg = b_ref[:, pl.ds(c0, D_STATE)]
        cb = lax.dot_general(cg.astype(BF16), bg.astype(BF16), (((1,), (1,)), ((), ())),
                             preferred_element_type=F32)
        bgt = bg.T
        yps = []
        for k in range(hpg // 2):
            x0 = pl.multiple_of(g * (hpg * HEAD_DIM) + k * V7X_LANES, V7X_LANES)
            x2 = x_ref[:, pl.ds(x0, V7X_LANES)]
            ms, ces, bws, ess = [], [], [], []
            for t in range(2):
                j = 2 * k + t
                acol = jnp.sum(lm * dag[j:j + 1, :], axis=1, keepdims=True)
                dec = jnp.where(keep, jnp.exp(acol - acg[j:j + 1, :]), 0.0)
                ms.append(cb * dec * dtg[j:j + 1, :])
                ces.append(cg * jnp.exp(acol))
                bws.append(bgt * wtg[j:j + 1, :])
                ess.append(jnp.broadcast_to(etg[j:j + 1, :], (D_STATE, V7X_LANES)))
            sp = st[g * (hpg // 2) + k]
            lhs = jnp.concatenate(ms + ces, axis=1).astype(BF16)
            rhs = jnp.concatenate([jnp.where(lane_lo, x2, 0.0), jnp.where(lane_lo, 0.0, x2), sp],
                                  axis=0).astype(BF16)
            yp = jnp.dot(lhs, rhs, preferred_element_type=F32)
            if has_prev:
                yp = yp + yp_ref[:, pl.ds(x0, V7X_LANES)]
            if gated:
                yps.append(yp)
            else:
                y_ref[:, pl.ds(x0, V7X_LANES)] = yp
            upd = jnp.dot(jnp.concatenate(bws, axis=0).astype(BF16), x2.astype(BF16),
                          preferred_element_type=F32)
            st[g * (hpg // 2) + k] = sp * jnp.concatenate(ess, axis=0) + jnp.where(diag, upd, 0.0)
        if gated:
            lanes = pl.ds(pl.multiple_of(g * (hpg * HEAD_DIM), hpg * HEAD_DIM), hpg * HEAD_DIM)
            v = (jnp.concatenate(yps, axis=1) + x_ref[:, lanes] * dsk_ref[:, lanes]) * _silu(z_ref[:, lanes])
            r = lax.rsqrt(jnp.mean(v * v, axis=-1, keepdims=True) + EPS)
            y_ref[:, lanes] = (v * r * nw_ref[:, lanes]).astype(y_ref.dtype)
        return carry

    lax.fori_loop(0, N_GROUPS, group, 0, unroll=True)

    @pl.when(step == pl.num_programs(0) - 1)
    def _():
        sf_ref[...] = st[...]


def ssd_scan_dir(xbc, dt_t, a_row, s0, y_prev, *, direction, gate=None):
    n = xbc.shape[0]
    nh = dt_t.shape[0] // 2
    d_inner = nh * HEAD_DIM
    gn = N_GROUPS * D_STATE
    assert n % CHUNK == 0 and d_inner % gn == 0
    nc = n // CHUNK
    rev = direction == 1
    cidx = (lambda i: nc - 1 - i) if rev else (lambda i: i)
    ins = [xbc, xbc, xbc, dt_t, a_row, s0]
    specs = [
        pl.BlockSpec((CHUNK, d_inner), lambda i: (cidx(i), 0)),
        pl.BlockSpec((CHUNK, gn), lambda i: (cidx(i), d_inner // gn)),
        pl.BlockSpec((CHUNK, gn), lambda i: (cidx(i), d_inner // gn + 1)),
        pl.BlockSpec((nh, CHUNK), lambda i: (direction, cidx(i))),
        pl.BlockSpec((nh, CHUNK), lambda i: (0, 0)),
        pl.BlockSpec(s0.shape, lambda i: (0, 0, 0)),
    ]
    if y_prev is not None:
        ins.append(y_prev)
        specs.append(pl.BlockSpec((CHUNK, d_inner), lambda i: (cidx(i), 0)))
    if gate is not None:
        zx, d_row, norm_w = gate
        vec = pl.BlockSpec((1, d_inner), lambda i: (0, 0))
        ins += [zx, d_row.reshape(1, d_inner), norm_w.reshape(1, d_inner)]
        specs += [pl.BlockSpec((CHUNK, d_inner), lambda i: (cidx(i), 0)), vec, vec]
    return pl.pallas_call(
        functools.partial(_scan_kernel, rev=rev, has_prev=y_prev is not None, gated=gate is not None),
        grid=(nc,),
        in_specs=specs,
        out_specs=[pl.BlockSpec((CHUNK, d_inner), lambda i: (cidx(i), 0)),
                   pl.BlockSpec(s0.shape, lambda i: (0, 0, 0))],
        out_shape=[jax.ShapeDtypeStruct((n, d_inner), F32 if gate is None else BF16),
                   jax.ShapeDtypeStruct(s0.shape, F32)],
        scratch_shapes=[pltpu.VMEM(s0.shape, F32), pltpu.VMEM((5, nh, CHUNK), F32)],
        compiler_params=_params("arbitrary"),
        name="scan_bwd" if rev else "scan_fwd",
    )(*ins)


def _row_tile(n, pref):
    return pref if n % pref == 0 else n


def ssd_layer(x, ctx_s, mod_l, mod_c, norm1, w_in, conv_w, conv_b, dt_bias, a_log, d_skip, norm_w, w_out, k, ctx_out):
    nh = a_log.shape[1]
    d_inner = nh * HEAD_DIM
    gn = N_GROUPS * D_STATE
    a_rows = jnp.broadcast_to(a_log.astype(F32)[:, :, None], (2, nh, CHUNK))
    d_row = jnp.repeat(d_skip.astype(F32), HEAD_DIM)
    in_tn = w_in.shape[-1] // 9

    def project(v, mod, width):
        n = v.shape[0]
        zx = matmul(v, w_in, w_index=k, mod=(norm1, mod[1], mod[0]), tm=_row_tile(n, 1024), tn=in_tn,
                    name="ssd_in")
        xbc = conv_silu(zx, conv_w, conv_b, width=width, col0=d_inner, tb=_row_tile(n, 1024))
        dt_t = dt_rows(zx, dt_bias, col0=2 * d_inner + 2 * gn, tb=_row_tile(n, 512))
        return zx, xbc, dt_t

    zx_c, xbc_c, dt_c = project(ctx_s, mod_c, ctx_s.shape[0])
    zx_l, xbc_l, dt_l = project(x, mod_l, GRID_W)
    zero = jnp.zeros((nh // 2, 2 * D_STATE, V7X_LANES), F32)
    yc, s_f = ssd_scan_dir(xbc_c, dt_c, a_rows[0], zero, None, direction=0)
    yl, _ = ssd_scan_dir(xbc_l, dt_l, a_rows[0], s_f, None, direction=0)
    yc, s_b = ssd_scan_dir(xbc_c, dt_c, a_rows[1], zero, yc, direction=1,
                           gate=(zx_c, d_row, norm_w) if ctx_out else None)
    yl, _ = ssd_scan_dir(xbc_l, dt_l, a_rows[1], s_b, yl, direction=1, gate=(zx_l, d_row, norm_w))

    def output(v, yn, gate):
        return matmul(yn, w_out, w_index=k, res=v, gate=gate, tm=_row_tile(v.shape[0], 512), tn=1024,
                      w_outer=True, name="ssd_out")

    x_new = output(x, yl, mod_l[2])
    ctx_new = output(ctx_s, yc, mod_c[2]) if ctx_out else None
    return x_new, ctx_new


def _dft_tables(n):
    j = jnp.arange(n, dtype=I32)
    ang = ((j[:, None] * j[None, :]) % n).astype(F32) * (2.0 * math.pi / n)
    return jnp.cos(ang), jnp.sin(ang)


def _fchan_kernel(x_ref, gain_ref, sc_ref, sh_ref, cs_ref, o_ref):
    h = _rms_mod(x_ref[...], gain_ref[...], sc_ref[...], sh_ref[...]).astype(BF16)
    gw = cs_ref.shape[0]
    d = h.shape[1]
    for g in range(d // gw):
        r = jnp.dot(h[:, g * gw:(g + 1) * gw], cs_ref[...], preferred_element_type=F32)
        o_ref[:, g * gw:(g + 1) * gw] = r[:, :gw]
        o_ref[:, d + g * gw:d + (g + 1) * gw] = r[:, gw:]


def _f1_kernel(r_ref, i_ref, k1_ref, tc_ref, ts_ref, yr_ref, yi_ref):
    la, sub, cw = r_ref.shape
    rows = la * sub
    xx = jnp.concatenate([r_ref[...].reshape(rows, cw), i_ref[...].reshape(rows, cw)], axis=0).astype(BF16)
    y = jnp.dot(k1_ref[...], xx, preferred_element_type=F32)
    yr, yi = y[:rows], y[rows:]
    c, s = tc_ref[...], ts_ref[...]
    yr_ref[...] = (c * yr + s * yi).reshape(la, sub, cw)
    yi_ref[...] = (c * yi - s * yr).reshape(la, sub, cw)


def _f2_kernel(r_ref, i_ref, k2_ref, o_ref):
    sub, lb, cw = r_ref.shape
    xx = jnp.concatenate([r_ref[...].reshape(sub * lb, cw), i_ref[...].reshape(sub * lb, cw)],
                         axis=0).astype(BF16)
    o_ref[...] = jnp.dot(k2_ref[...], xx, preferred_element_type=F32).reshape(lb, sub, cw)


def fnet_layer(x, mod, norm1, w_out, k, *, cw=512):
    l, d = x.shape
    sub = V7X_SUBLANES
    la = l // 128 if l >= 1024 else sub
    lb = l // la
    gw = d // FNET_GROUPS
    assert la % sub == 0 and lb % sub == 0 and d % cw == 0
    shift, scale, gate = mod
    cc, sc = _dft_tables(gw)
    norm = 1.0 / math.sqrt(l * gw)
    cs = (jnp.concatenate([cc, -sc], axis=1) * norm).astype(BF16)
    tm = _row_tile(l, 512)
    vec = pl.BlockSpec((1, d), lambda i: (0, 0))
    g2 = pl.pallas_call(
        _fchan_kernel,
        grid=(l // tm,),
        in_specs=[pl.BlockSpec((tm, d), lambda i: (i, 0)), vec, vec, vec, pl.BlockSpec((gw, 2 * gw), lambda i: (0, 0))],
        out_specs=pl.BlockSpec((tm, 2 * d), lambda i: (i, 0)),
        out_shape=jax.ShapeDtypeStruct((l, 2 * d), F32),
        compiler_params=_params("arbitrary"),
        name="fnet_chan",
    )(x, norm1.reshape(1, d), scale.reshape(1, d), shift.reshape(1, d), cs)
    ca, sa = _dft_tables(la)
    eye = jnp.eye(sub, dtype=F32)
    k1 = jnp.concatenate([jnp.concatenate([jnp.kron(ca, eye), jnp.kron(sa, eye)], axis=1),
                          jnp.concatenate([jnp.kron(-sa, eye), jnp.kron(ca, eye)], axis=1)], axis=0).astype(BF16)
    bi = jnp.arange(lb, dtype=I32).reshape(lb // sub, 1, sub)
    di = jnp.arange(la, dtype=I32).reshape(1, la, 1)
    ang = ((bi * di) % l).astype(F32) * (2.0 * math.pi / l)
    tc = jnp.cos(ang).reshape(lb // sub, la * sub, 1)
    ts = jnp.sin(ang).reshape(lb // sub, la * sub, 1)
    g3 = g2.reshape(la, lb, 2 * d)
    ncb = d // cw
    blk1 = lambda off: pl.BlockSpec((la, sub, cw), lambda b, c: (0, b, off + c))
    twb = pl.BlockSpec((None, la * sub, 1), lambda b, c: (b, 0, 0))
    yr3, yi3 = pl.pallas_call(
        _f1_kernel,
        grid=(lb // sub, ncb),
        in_specs=[blk1(0), blk1(ncb), pl.BlockSpec(k1.shape, lambda b, c: (0, 0)), twb, twb],
        out_specs=[blk1(0), blk1(0)],
        out_shape=[jax.ShapeDtypeStruct((la, lb, d), F32)] * 2,
        compiler_params=_params("arbitrary", "arbitrary"),
        name="fnet_stage1",
    )(g3, g3, k1, tc, ts)
    cb_, sb_ = _dft_tables(lb)
    k2 = jnp.concatenate([jnp.einsum("cb,de->cdeb", cb_, eye).reshape(lb * sub, sub * lb),
                          jnp.einsum("cb,de->cdeb", sb_, eye).reshape(lb * sub, sub * lb)], axis=1).astype(BF16)
    blk2 = pl.BlockSpec((sub, lb, cw), lambda dd, c: (dd, 0, c))
    f3 = pl.pallas_call(
        _f2_kernel,
        grid=(la // sub, ncb),
        in_specs=[blk2, blk2, pl.BlockSpec(k2.shape, lambda dd, c: (0, 0))],
        out_specs=pl.BlockSpec((lb, sub, cw), lambda dd, c: (0, dd, c)),
        out_shape=jax.ShapeDtypeStruct((lb, la, d), F32),
        compiler_params=_params("arbitrary", "arbitrary"),
        name="fnet_stage2",
    )(yr3, yi3, k2)
    return matmul(f3.reshape(l, d), w_out, w_index=k, res=x, gate=gate, tm=_row_tile(l, 512), tn=d,
                  name="fnet_out")


ROUTE_S = 128
ROUTE_TOKENS = ROUTE_S * ROUTE_S


def _moe_prep_kernel(x_ref, gain_ref, sc_ref, sh_ref, wr_ref, h_ref, lg_ref):
    h = _rms_mod(x_ref[...], gain_ref[...], sc_ref[...], sh_ref[...])
    tb = h.shape[0]
    for j in range(TOK_ROWS):
        h_ref[pl.ds(j, tb, stride=TOK_ROWS), :] = h[:, j * V7X_LANES:(j + 1) * V7X_LANES]
    lg_ref[...] = lax.dot_general(wr_ref[...], h, (((1,), (1,)), ((), ())), precision=HIGHEST,
                                  preferred_element_type=F32)


def moe_prep(x, norm2, scale, shift, w_router, *, tb):
    n, d = x.shape
    e = w_router.shape[1]
    assert d == TOK_ROWS * V7X_LANES and n % tb == 0
    vec = pl.BlockSpec((1, d), lambda i: (0, 0))
    return pl.pallas_call(
        _moe_prep_kernel,
        grid=(n // tb,),
        in_specs=[pl.BlockSpec((tb, d), lambda i: (i, 0)), vec, vec, vec, pl.BlockSpec((e, d), lambda i: (0, 0))],
        out_specs=[pl.BlockSpec((tb * TOK_ROWS, V7X_LANES), lambda i: (i, 0)), pl.BlockSpec((e, tb), lambda i: (0, i))],
        out_shape=[jax.ShapeDtypeStruct((n * TOK_ROWS, V7X_LANES), F32), jax.ShapeDtypeStruct((e, n), F32)],
        compiler_params=_params("arbitrary"),
        name="moe_prep",
    )(x, norm2.reshape(1, d), scale.reshape(1, d), shift.reshape(1, d), w_router.T)


def _tok_cumsum(m, lincl_bf16, ustrict):
    wc = jnp.dot(lincl_bf16, m.astype(BF16), preferred_element_type=F32)
    coltot = wc[ROUTE_S - 1:ROUTE_S, :]
    colpref = jnp.dot(jnp.broadcast_to(coltot, (V7X_SUBLANES, ROUTE_S)), ustrict, precision=HIGHEST,
                      preferred_element_type=F32)[0:1]
    return wc, coltot, colpref


def _route_kernel(lg_ref, idx_ref, q_ref, g_ref, off_ref, cnt_ref, aff_scr, bits_scr, sel_scr, qt_scr, *,
                  n_valid, cap, pc):
    ne = lg_ref.shape[0]
    s_ = ROUTE_S
    c_pad = idx_ref.shape[2]
    si = lax.broadcasted_iota(I32, (s_, s_), 0)
    ji = lax.broadcasted_iota(I32, (s_, s_), 1)
    valid = ji * s_ + si < n_valid
    lincl = (ji <= si).astype(BF16)
    ustrict = (si < ji).astype(F32)
    ones = jnp.ones((s_, s_), BF16)

    ls = [lg_ref[e] for e in range(ne)]
    mx = functools.reduce(jnp.maximum, ls)
    ex = [jnp.exp(l - mx) for l in ls]
    den = functools.reduce(jnp.add, ex)
    for e in range(ne):
        aff = jnp.where(valid, ex[e] / den, -1.0)
        aff_scr[e] = aff
        bits_scr[e] = pltpu.bitcast(aff, I32)

    def bit_step(i, ts):
        bit = jnp.left_shift(jnp.int32(1), 30 - i)
        out = []
        for e in range(ne):
            cand = ts[e] | bit
            ge = (bits_scr[e] >= cand).astype(F32)
            c1 = jnp.sum(jnp.sum(ge, axis=0, keepdims=True), axis=1, keepdims=True)
            out.append(jnp.where(c1 >= cap, cand, ts[e]))
        return tuple(out)

    ts = lax.fori_loop(0, 31, bit_step, tuple(jnp.zeros((1, 1), I32) for _ in range(ne)))

    cnt = jnp.zeros((s_, s_), F32)
    for e in range(ne):
        bits = bits_scr[e]
        gt = bits > ts[e]
        eq = (bits == ts[e]).astype(F32)
        n_gt = jnp.sum(jnp.sum(gt.astype(F32), axis=0, keepdims=True), axis=1, keepdims=True)
        wc, _, colpref = _tok_cumsum(eq, lincl, ustrict)
        rank_eq = wc + colpref - eq
        sel = jnp.where(gt | ((eq > 0.0) & (rank_eq < cap - n_gt)), 1.0, 0.0)
        sel_scr[e] = sel
        qt_scr[e] = cnt
        cnt = cnt + sel
    wc, _, colpref = _tok_cumsum(cnt, lincl, ustrict)
    offs = wc + colpref - cnt
    off_ref[...] = offs
    cnt_ref[...] = cnt

    def per_expert(e, carry):
        m = sel_scr[e]
        qt = qt_scr[e] + offs
        aff = aff_scr[e]
        wc, coltot, colpref = _tok_cumsum(m, lincl, ustrict)
        wc_b = wc.astype(BF16)
        colcum = jnp.dot((lincl.astype(F32) * coltot).astype(BF16), ones, preferred_element_type=F32)
        colpref8 = jnp.broadcast_to(colpref, (V7X_SUBLANES, s_))
        sub = lax.broadcasted_iota(I32, (s_, pc), 0).astype(F32)
        reps = pc // s_
        colcum_t = jnp.concatenate([colcum] * reps, axis=1) if reps > 1 else colcum
        for c0 in range(0, c_pad, pc):
            p = (lax.broadcasted_iota(I32, (s_, pc), 1) + c0).astype(F32)
            p_row = p[0:1]
            blk = jnp.sum((colcum_t <= p).astype(F32), axis=0, keepdims=True)
            oh_j = (sub == blk).astype(F32)
            colvec = jnp.dot(wc_b, oh_j.astype(BF16), preferred_element_type=F32)
            cp = jnp.dot(colpref8, oh_j, precision=HIGHEST, preferred_element_type=F32)[0:1]
            s_idx = jnp.sum((colvec <= p_row - cp).astype(F32), axis=0, keepdims=True)
            oh_s = (sub == s_idx).astype(F32)
            gv = jnp.sum(oh_s * jnp.dot(aff, oh_j, precision=HIGHEST, preferred_element_type=F32),
                         axis=0, keepdims=True)
            qv = jnp.sum(oh_s * jnp.dot(qt, oh_j, precision=HIGHEST, preferred_element_type=F32),
                         axis=0, keepdims=True)
            live = p_row < cap
            idx_ref[e, :, c0:c0 + pc] = jnp.where(live, blk * s_ + s_idx, 0.0).astype(I32)
            q_ref[e, :, c0:c0 + pc] = jnp.where(live, qv, 0.0).astype(I32)
            g_ref[e, :, c0:c0 + pc] = jnp.where(live, gv, 0.0)
        return carry

    lax.fori_loop(0, ne, per_expert, 0)


def moe_route(logits_t, *, cap):
    ne, n = logits_t.shape
    assert n <= ROUTE_TOKENS and cap <= n
    s_ = ROUTE_S
    c_pad = max(cap, s_)
    pc = min(c_pad, 2 * s_)
    assert c_pad % pc == 0
    lg = jnp.pad(logits_t, ((0, 0), (0, ROUTE_TOKENS - n))).reshape(ne, s_, s_).transpose(0, 2, 1)
    full = lambda shape: pl.BlockSpec(shape, lambda i: (0,) * len(shape))
    slot = jax.ShapeDtypeStruct((ne, 1, c_pad), I32)
    idx, q, g, offs, cnt = pl.pallas_call(
        functools.partial(_route_kernel, n_valid=n, cap=cap, pc=pc),
        grid=(1,),
        in_specs=[full((ne, s_, s_))],
        out_specs=[full((ne, 1, c_pad))] * 3 + [full((s_, s_))] * 2,
        out_shape=[slot, slot, jax.ShapeDtypeStruct((ne, 1, c_pad), F32),
                   jax.ShapeDtypeStruct((s_, s_), F32), jax.ShapeDtypeStruct((s_, s_), F32)],
        scratch_shapes=[pltpu.VMEM((ne, s_, s_), F32), pltpu.VMEM((ne, s_, s_), I32),
                        pltpu.VMEM((ne, s_, s_), F32), pltpu.VMEM((ne, s_, s_), F32)],
        compiler_params=_params("arbitrary"),
        name="moe_route",
    )(lg)
    tok = lambda a: a.T.reshape(-1)[:n]
    return idx[:, 0, :cap], q[:, 0, :cap], g[:, 0, :cap], tok(offs), tok(cnt)


def _ffn_kernel(idx_ref, q_ref, g_ref, h_hbm, wg_ref, wu_ref, wd_ref, z_hbm, ga, gb, sa, sb, sems):
    step = pl.program_id(0)
    last = pl.num_programs(0) - 1
    cbh = ga.shape[0] // TOK_ROWS
    total = idx_ref.shape[0]
    base = step * (2 * cbh)
    srows = lambda p: pl.ds(p * TOK_ROWS, TOK_ROWS)
    drows = lambda r: pl.ds(pl.multiple_of(r * TOK_ROWS, TOK_ROWS), TOK_ROWS)
    G_A, G_B, S_A, S_B = range(4)

    def gather_start(buf, sem, slot0):
        for p in range(cbh):
            pltpu.make_async_copy(h_hbm.at[drows(idx_ref[slot0 + p]), :], buf.at[srows(p), :],
                                  sems.at[sem]).start(priority=p % 2)

    def gather_wait(buf, sem):
        for p in range(cbh):
            pltpu.make_async_copy(h_hbm.at[srows(0), :], buf.at[srows(p), :], sems.at[sem]).wait()

    def scatter_start(buf, sem, slot0):
        for p in range(cbh):
            pltpu.make_async_copy(buf.at[srows(p), :], z_hbm.at[drows(q_ref[slot0 + p]), :],
                                  sems.at[sem]).start(priority=p % 2)

    def scatter_wait(buf, sem):
        for p in range(cbh):
            pltpu.make_async_copy(buf.at[srows(p), :], z_hbm.at[srows(0), :], sems.at[sem]).wait()

    def compute(gbuf, sbuf, g):
        x = jnp.concatenate([gbuf[pl.ds(j, cbh, stride=TOK_ROWS), :] for j in range(TOK_ROWS)],
                            axis=1).astype(BF16)
        a = jnp.dot(x, wg_ref[...], preferred_element_type=F32)
        u = jnp.dot(x, wu_ref[...], preferred_element_type=F32)
        y = jnp.dot((_silu(a) * u).astype(BF16), wd_ref[...], preferred_element_type=F32) * g
        for j in range(TOK_ROWS):
            sbuf[pl.ds(j, cbh, stride=TOK_ROWS), :] = y[:, j * V7X_LANES:(j + 1) * V7X_LANES]

    @pl.when(step == 0)
    def _():
        gather_start(ga, G_A, 0)

    gather_wait(ga, G_A)

    @pl.when(step > 0)
    def _():
        scatter_wait(sa, S_A)

    gather_start(gb, G_B, base + cbh)
    compute(ga, sa, g_ref[0:cbh, :])
    scatter_start(sa, S_A, base)
    gather_wait(gb, G_B)

    @pl.when(step > 0)
    def _():
        scatter_wait(sb, S_B)

    gather_start(ga, G_A, jnp.minimum(base + 2 * cbh, total - cbh))
    compute(gb, sb, g_ref[cbh:2 * cbh, :])
    scatter_start(sb, S_B, base + cbh)

    @pl.when(step == last)
    def _():
        gather_wait(ga, G_A)
        scatter_wait(sa, S_A)
        scatter_wait(sb, S_B)


def moe_ffn(h_rows, idx, q, g, w_gate, w_up, w_down, layer, *, cbh):
    ne, cap = idx.shape
    d, f = w_gate.shape[2:]
    sb_rows = 2 * cbh
    assert cap % sb_rows == 0
    spe = cap // sb_rows
    wmap = lambda s, *_: (layer, s // spe, 0, 0)
    buf = pltpu.VMEM((cbh * TOK_ROWS, V7X_LANES), F32)
    return pl.pallas_call(
        _ffn_kernel,
        grid_spec=pltpu.PrefetchScalarGridSpec(
            num_scalar_prefetch=2,
            grid=(ne * spe,),
            in_specs=[
                pl.BlockSpec((sb_rows, 1), lambda s, *_: (s, 0)),
                pl.BlockSpec(memory_space=pl.ANY),
                pl.BlockSpec((None, None, d, f), wmap),
                pl.BlockSpec((None, None, d, f), wmap),
                pl.BlockSpec((None, None, f, d), wmap),
            ],
            out_specs=pl.BlockSpec(memory_space=pl.ANY),
            scratch_shapes=[buf, buf, buf, buf, pltpu.SemaphoreType.DMA((4,))],
        ),
        out_shape=jax.ShapeDtypeStruct((ne * cap * TOK_ROWS, V7X_LANES), F32),
        compiler_params=_params("arbitrary"),
        name="moe_ffn",
    )(idx.reshape(-1), q.reshape(-1), g.reshape(ne * cap, 1), h_rows, w_gate, w_up, w_down)


def _combine_kernel(boff_ref, x_ref, gate_ref, off_ref, cnt_ref, z_hbm, *rest, ch, p_total, final):
    fw_ref = rest[0] if final else None
    o_ref, z0, z1, sems = rest[1:] if final else rest
    i = pl.program_id(0)
    start = boff_ref[i]
    end = boff_ref[i + 1]
    nch = (end - start + ch - 1) // ch
    lo = off_ref[...]
    hi = lo + cnt_ref[...]
    o_ref[...] = jnp.zeros(o_ref.shape, F32)
    zbufs = (z0, z1)

    def row_start(c, first_row=None):
        first_row = start if first_row is None else first_row
        return jnp.minimum(first_row + c * ch, p_total - ch)

    def copy(c, slot, first_row=None):
        src = z_hbm.at[pl.ds(pl.multiple_of(row_start(c, first_row) * TOK_ROWS, TOK_ROWS), ch * TOK_ROWS), :]
        return pltpu.make_async_copy(src, zbufs[slot], sems.at[slot])

    def process(c, slot):
        copy(c, slot).wait()

        @pl.when(c + 1 < nch)
        def _():
            copy(c + 1, 1 - slot).start()

        first = start + c * ch
        qabs = row_start(c) + lax.broadcasted_iota(I32, (1, ch), 1)
        qf = qabs.astype(F32)
        seg = ((qf >= lo) & (qf < hi) & (qabs >= first)).astype(BF16)
        zb = zbufs[slot]
        z = jnp.concatenate([zb[pl.ds(j, ch, stride=TOK_ROWS), :] for j in range(TOK_ROWS)], axis=1)
        o_ref[...] += jnp.dot(seg, z.astype(BF16), preferred_element_type=F32)

    @pl.when((i == 0) & (nch > 0))
    def _():
        copy(0, 0).start()

    def pair(k, carry):
        process(2 * k, 0)

        @pl.when(2 * k + 1 < nch)
        def _():
            process(2 * k + 1, 1)

        return carry

    lax.fori_loop(0, (nch + 1) // 2, pair, 0)

    @pl.when(i + 1 < pl.num_programs(0))
    def _():
        @pl.when(boff_ref[i + 2] > end)
        def _():
            copy(0, 0, first_row=end).start()

    out = x_ref[...] + gate_ref[...] * o_ref[...]
    if final:
        out = out * lax.rsqrt(jnp.mean(out * out, axis=-1, keepdims=True) + EPS) * fw_ref[...]
    o_ref[...] = out


def moe_combine(x, gate, z_rows, offs, cnt, *, tb, ch=256, final_w=None):
    n, d = x.shape
    p_total = z_rows.shape[0] // TOK_ROWS
    assert n % tb == 0 and p_total >= ch
    boff = jnp.concatenate([offs[::tb], jnp.full((1,), p_total, F32)]).astype(I32)
    col = pl.BlockSpec((tb, 1), lambda i, s: (i, 0))
    vec = pl.BlockSpec((1, d), lambda i, s: (0, 0))
    extra_in, extra_spec = ([final_w.reshape(1, d)], [vec]) if final_w is not None else ([], [])
    return pl.pallas_call(
        functools.partial(_combine_kernel, ch=ch, p_total=p_total, final=final_w is not None),
        grid_spec=pltpu.PrefetchScalarGridSpec(
            num_scalar_prefetch=1,
            grid=(n // tb,),
            in_specs=[pl.BlockSpec((tb, d), lambda i, s: (i, 0)), vec,
                      col, col, pl.BlockSpec(memory_space=pl.ANY)] + extra_spec,
            out_specs=pl.BlockSpec((tb, d), lambda i, s: (i, 0)),
            scratch_shapes=[pltpu.VMEM((ch * TOK_ROWS, V7X_LANES), F32), pltpu.VMEM((ch * TOK_ROWS, V7X_LANES), F32),
                            pltpu.SemaphoreType.DMA((2,))],
        ),
        out_shape=jax.ShapeDtypeStruct((n, d), F32),
        compiler_params=_params("arbitrary"),
        name="moe_combine",
    )(boff, x, gate.reshape(1, d), offs.reshape(n, 1), cnt.reshape(n, 1), z_rows, *extra_in)


def moe_layer(x, mod, norm2, w_router, w_gate, w_up, w_down, layer, final_w=None):
    n = x.shape[0]
    ne = w_router.shape[1]
    cap = (CAPACITY_FACTOR * n) // ne
    shift, scale, gate = mod
    h_rows, logits_t = moe_prep(x, norm2, scale, shift, w_router, tb=_row_tile(n, 256))
    idx, q, g, offs, cnt = moe_route(logits_t, cap=cap)
    z_rows = moe_ffn(h_rows, idx, q, g, w_gate, w_up, w_down, layer, cbh=min(cap // 2, 256))
    return moe_combine(x, gate, z_rows, offs, cnt, tb=_row_tile(n, 512), final_w=final_w)


def kernel(x, c, ctx, c_ctx, ada_w, ada_b, norm1_w, norm2_w, final_norm_w, ssd_w_in, ssd_conv_w, ssd_conv_b,
           ssd_dt_bias, ssd_a_log, ssd_d, ssd_norm_w, ssd_w_out, fnet_w_out, moe_w_router, moe_w_gate, moe_w_up,
           moe_w_down):
    depth = ada_w.shape[0]
    d = x.shape[-1]
    assert x.shape[0] == 1, "one sample per call"
    xs, cs = x[0], ctx[0]
    ada = ada_all(c, c_ctx, ada_w, ada_b)
    moe_w = (moe_w_gate.astype(BF16), moe_w_up.astype(BF16), moe_w_down.astype(BF16))
    w_in_b, w_out_b, w_fnet_b = ssd_w_in.astype(BF16), ssd_w_out.astype(BF16), fnet_w_out.astype(BF16)
    for i in range(depth):
        k = i // 2
        is_ssd = i % 2 == 0
        ctx_later = any(j % 2 == 0 for j in range(i + 1, depth))
        sh1, sc1, g1, sh2, sc2, g2 = (ada[i, 0, m * d:(m + 1) * d] for m in range(6))
        csh1, csc1, cg1, csh2, csc2, cg2 = (ada[i, 1, m * d:(m + 1) * d] for m in range(6))
        if is_ssd:
            xs, cs_new = ssd_layer(xs, cs, (sh1, sc1, g1), (csh1, csc1, cg1), norm1_w[i], w_in_b, ssd_conv_w[k],
                                   ssd_conv_b[k], ssd_dt_bias[k], ssd_a_log[k], ssd_d[k], ssd_norm_w[k],
                                   w_out_b, k, ctx_later)
        else:
            xs = fnet_layer(xs, (sh1, sc1, g1), norm1_w[i], w_fnet_b, k)
            cs_new = fnet_layer(cs, (csh1, csc1, cg1), norm1_w[i], w_fnet_b, k) if ctx_later else None
        xs = moe_layer(xs, (sh2, sc2, g2), norm2_w[i], moe_w_router[i], *moe_w, i,
                       final_w=final_norm_w if i == depth - 1 else None)
        if ctx_later:
            cs = moe_layer(cs_new, (csh2, csc2, cg2), norm2_w[i], moe_w_router[i], *moe_w, i)
    return xs[None]
```

```python
import functools
import math

import jax
import jax.numpy as jnp
import numpy as np
from jax import lax
from jax.experimental import pallas as pl
from jax.experimental.pallas import tpu as pltpu

F32 = jnp.float32
BF16 = jnp.bfloat16
I32 = jnp.int32
HIGHEST = lax.Precision.HIGHEST

V7X_LANES = 128
V7X_SUBLANES = 8
V7X_VMEM_BYTES = 64 * 1024 * 1024
VMEM_LIMIT = V7X_VMEM_BYTES - 8 * 1024 * 1024

GRID_W = 64
HEAD_DIM = 64
N_GROUPS = 8
D_STATE = 128
CHUNK = 128
FNET_GROUPS = 8
N_EXPERTS = 16
CAPACITY_FACTOR = 2
EPS = 1e-6
TOK_ROWS = 16


def _params(*sem):
    return pltpu.CompilerParams(dimension_semantics=sem, vmem_limit_bytes=VMEM_LIMIT)


def _rms_mod(x, gain, scale, shift):
    y = x * lax.rsqrt(jnp.mean(x * x, axis=-1, keepdims=True) + EPS)
    return y * gain * (1.0 + scale) + shift


def _silu(x):
    return x * jax.nn.sigmoid(x)


def _ada_kernel(c_ref, w_ref, b_ref, o_ref):
    w = w_ref[...]
    rows = []
    for r in range(2):
        s = _silu(c_ref[r])
        rows.append(jnp.sum(s * w, axis=0, keepdims=True) + b_ref[...])
    rows.append(jnp.zeros((V7X_SUBLANES - 2, w.shape[1]), F32))
    o_ref[...] = jnp.concatenate(rows, axis=0)


def ada_all(c, c_ctx, ada_w, ada_b):
    depth, k, n6 = ada_w.shape
    tn = n6 // 8
    cond = jnp.stack([c[0], c_ctx], axis=0)[:, :, None]
    out = pl.pallas_call(
        _ada_kernel,
        grid=(depth, n6 // tn),
        in_specs=[
            pl.BlockSpec((2, k, 1), lambda l, n: (0, 0, 0)),
            pl.BlockSpec((None, k, tn), lambda l, n: (l, 0, n)),
            pl.BlockSpec((None, 1, tn), lambda l, n: (l, 0, n)),
        ],
        out_specs=pl.BlockSpec((None, V7X_SUBLANES, tn), lambda l, n: (l, 0, n)),
        out_shape=jax.ShapeDtypeStruct((depth, V7X_SUBLANES, n6), F32),
        compiler_params=_params("arbitrary", "arbitrary"),
        name="ada",
    )(cond, ada_w, ada_b[:, None, :])
    return out


def _mm_kernel(*refs, prologue, epilogue):
    it = iter(refs)
    a_ref = next(it)
    if prologue:
        gain_ref, sc_ref, sh_ref = next(it), next(it), next(it)
    w_ref = next(it)
    if epilogue:
        res_ref, gate_ref = next(it), next(it)
    o_ref = next(it)
    a_scr = next(it, None)

    if a_scr is None:
        a_bf16 = a_ref[...]
    else:
        @pl.when(pl.program_id(1) == 0)
        def _():
            a = a_ref[...]
            if prologue:
                a = _rms_mod(a, gain_ref[...], sc_ref[...], sh_ref[...])
            a_scr[...] = a.astype(BF16)

        a_bf16 = a_scr[...]
    acc = jnp.dot(a_bf16, w_ref[...], preferred_element_type=F32)
    if epilogue:
        acc = res_ref[...] + gate_ref[...] * acc
    o_ref[...] = acc.astype(o_ref.dtype)


def matmul(a, w_bf16, *, w_index=None, mod=None, res=None, gate=None, tm, tn, out_dtype=F32, name="mm",
           w_outer=False):
    m, k = a.shape
    n = w_bf16.shape[-1]
    assert m % tm == 0 and n % tn == 0
    use_scr = mod is not None or a.dtype != BF16
    assert not (w_outer and use_scr)
    ij = (lambda g0, g1: (g1, g0)) if w_outer else (lambda g0, g1: (g0, g1))
    row = lambda g0, g1: (0, 0)
    ins, specs = [a], [pl.BlockSpec((tm, k), lambda g0, g1: (ij(g0, g1)[0], 0))]
    if mod is not None:
        for v in mod:
            ins.append(v.reshape(1, k))
            specs.append(pl.BlockSpec((1, k), row))
    ins.append(w_bf16)
    if w_index is None:
        specs.append(pl.BlockSpec((k, tn), lambda g0, g1: (0, ij(g0, g1)[1])))
    else:
        specs.append(pl.BlockSpec((None, k, tn), lambda g0, g1: (w_index, 0, ij(g0, g1)[1])))
    if res is not None:
        ins += [res, gate.reshape(1, n)]
        specs += [pl.BlockSpec((tm, tn), ij), pl.BlockSpec((1, tn), lambda g0, g1: (0, ij(g0, g1)[1]))]
    return pl.pallas_call(
        functools.partial(_mm_kernel, prologue=mod is not None, epilogue=res is not None),
        grid=(n // tn, m // tm) if w_outer else (m // tm, n // tn),
        in_specs=specs,
        out_specs=pl.BlockSpec((tm, tn), ij),
        out_shape=jax.ShapeDtypeStruct((m, n), out_dtype),
        scratch_shapes=[pltpu.VMEM((tm, k), BF16)] if use_scr else [],
        compiler_params=_params("arbitrary", "arbitrary"),
        name=name,
    )(*ins)


def _conv_kernel(prev_ref, cur_ref, next_ref, w_ref, b_ref, o_ref, *, width):
    i = pl.program_id(0)
    last = pl.num_programs(0) - 1
    cur = cur_ref[...]
    tb, cb = cur.shape
    prev = jnp.where(i > 0, prev_ref[...], 0.0)
    nxt = jnp.where(i < last, next_ref[...], 0.0)
    if tb > width:
        up = jnp.concatenate([prev, cur[: tb - width]], axis=0)
        dn = jnp.concatenate([cur[width:], nxt], axis=0)
    else:
        up, dn = prev, nxt
    w = w_ref[...]

    def tap(dw):
        return up * w[dw:dw + 1] + cur * w[3 + dw:4 + dw] + dn * w[6 + dw:7 + dw]

    col = lax.broadcasted_iota(I32, (tb, cb), 0) & (width - 1)
    left = jnp.where(col == 0, 0.0, pltpu.roll(tap(0), 1, axis=0))
    right = jnp.where(col == width - 1, 0.0, pltpu.roll(tap(2), tb - 1, axis=0))
    o_ref[...] = _silu(left + tap(1) + right + b_ref[...])


def conv_silu(zx, conv_w, conv_b, *, width, col0, tb, cb=512):
    n = zx.shape[0]
    c = conv_w.shape[-1]
    assert width & (width - 1) == 0 and tb % width == 0 and n % tb == 0 and c % cb == 0 and col0 % cb == 0
    cblk0 = col0 // cb
    per = tb // width
    nrow = n // width
    return pl.pallas_call(
        functools.partial(_conv_kernel, width=width),
        grid=(n // tb, c // cb),
        in_specs=[
            pl.BlockSpec((width, cb), lambda i, j: (jnp.maximum(i * per - 1, 0), cblk0 + j)),
            pl.BlockSpec((tb, cb), lambda i, j: (i, cblk0 + j)),
            pl.BlockSpec((width, cb), lambda i, j: (jnp.minimum((i + 1) * per, nrow - 1), cblk0 + j)),
            pl.BlockSpec((9, cb), lambda i, j: (0, j)),
            pl.BlockSpec((1, cb), lambda i, j: (0, j)),
        ],
        out_specs=pl.BlockSpec((tb, cb), lambda i, j: (i, j)),
        out_shape=jax.ShapeDtypeStruct((n, c), F32),
        compiler_params=_params("arbitrary", "arbitrary"),
        name="conv",
    )(zx, zx, zx, conv_w.reshape(9, c), conv_b.reshape(1, c))


def _dt_kernel(x_ref, b_ref, o_ref):
    v = x_ref[...] + b_ref[...]
    sp = jnp.maximum(v, 0.0) + jnp.log1p(jnp.exp(-jnp.abs(v)))
    o_ref[...] = sp.T


def dt_rows(zx, dt_bias, *, col0, tb):
    n = zx.shape[0]
    w = dt_bias.size
    assert col0 % w == 0 and n % tb == 0
    return pl.pallas_call(
        _dt_kernel,
        grid=(n // tb,),
        in_specs=[pl.BlockSpec((tb, w), lambda i: (i, col0 // w)), pl.BlockSpec((1, w), lambda i: (0, 0))],
        out_specs=pl.BlockSpec((w, tb), lambda i: (0, i)),
        out_shape=jax.ShapeDtypeStruct((w, n), F32),
        compiler_params=_params("arbitrary"),
        name="dt",
    )(zx, dt_bias.reshape(1, w))


def _scan_kernel(x_ref, b_ref, c_ref, dt_ref, a_ref, s0_ref, *rest, rev, has_prev, gated):
    rest = list(rest)
    yp_ref = rest.pop(0) if has_prev else None
    z_ref, dsk_ref, nw_ref = (rest.pop(0), rest.pop(0), rest.pop(0)) if gated else (None, None, None)
    y_ref, sf_ref, st, rows = rest
    step = pl.program_id(0)
    q = CHUNK
    nh = dt_ref.shape[0]
    hpg = nh // N_GROUPS
    half = V7X_LANES // 2

    @pl.when(step == 0)
    def _():
        st[...] = s0_ref[...]

    ri = lax.broadcasted_iota(I32, (q, q), 0)
    ci = lax.broadcasted_iota(I32, (q, q), 1)
    if rev:
        upper, keep = ri >= ci, ri <= ci
    else:
        upper, keep = ri <= ci, ri >= ci
    lm = keep.astype(F32)
    dt = dt_ref[...]
    da = dt * -jnp.exp(a_ref[...])
    acs = jnp.dot(da, upper.astype(F32), precision=HIGHEST, preferred_element_type=F32)
    tot = jnp.sum(da, axis=1, keepdims=True)
    rows[0] = dt
    rows[1] = da
    rows[2] = acs
    rows[3] = dt * jnp.exp(tot - acs)
    rows[4] = jnp.broadcast_to(jnp.exp(tot), (nh, q))

    lane_lo = lax.broadcasted_iota(I32, (q, V7X_LANES), 1) < half
    r2 = lax.broadcasted_iota(I32, (2 * D_STATE, V7X_LANES), 0) < D_STATE
    l2 = lax.broadcasted_iota(I32, (2 * D_STATE, V7X_LANES), 1) < half
    diag = r2 == l2

    def group(g, carry):
        r0 = pl.multiple_of(g * hpg, hpg)
        dtg = rows[0, pl.ds(r0, hpg), :]
        dag = rows[1, pl.ds(r0, hpg), :]
        acg = rows[2, pl.ds(r0, hpg), :]
        wtg = rows[3, pl.ds(r0, hpg), :]
        etg = rows[4, pl.ds(r0, hpg), :]
        c0 = pl.multiple_of(g * D_STATE, D_STATE)
        cg = c_ref[:, pl.ds(c0, D_STATE)]
        bg = b_ref[:, pl.ds(c0, D_STATE)]
        cb = lax.dot_general(cg.astype(BF16), bg.astype(BF16), (((1,), (1,)), ((), ())),
                             preferred_element_type=F32)
        bgt = bg.T
        yps = []
        for k in range(hpg // 2):
            x0 = pl.multiple_of(g * (hpg * HEAD_DIM) + k * V7X_LANES, V7X_LANES)
            x2 = x_ref[:, pl.ds(x0, V7X_LANES)]
            ms, ces, bws, ess = [], [], [], []
            for t in range(2):
                j = 2 * k + t
                acol = jnp.sum(lm * dag[j:j + 1, :], axis=1, keepdims=True)
                dec = jnp.where(keep, jnp.exp(acol - acg[j:j + 1, :]), 0.0)
                ms.append(cb * dec * dtg[j:j + 1, :])
                ces.append(cg * jnp.exp(acol))
                bws.append(bgt * wtg[j:j + 1, :])
                ess.append(jnp.broadcast_to(etg[j:j + 1, :], (D_STATE, V7X_LANES)))
            sp = st[g * (hpg // 2) + k]
            lhs = jnp.concatenate(ms + ces, axis=1).astype(BF16)
            rhs = jnp.concatenate([jnp.where(lane_lo, x2, 0.0), jnp.where(lane_lo, 0.0, x2), sp],
                                  axis=0).astype(BF16)
            yp = jnp.dot(lhs, rhs, preferred_element_type=F32)
            if has_prev:
                yp = yp + yp_ref[:, pl.ds(x0, V7X_LANES)]
            if gated:
                yps.append(yp)
            else:
                y_ref[:, pl.ds(x0, V7X_LANES)] = yp
            upd = jnp.dot(jnp.concatenate(bws, axis=0).astype(BF16), x2.astype(BF16),
                          preferred_element_type=F32)
            st[g * (hpg // 2) + k] = sp * jnp.concatenate(ess, axis=0) + jnp.where(diag, upd, 0.0)
        if gated:
            lanes = pl.ds(pl.multiple_of(g * (hpg * HEAD_DIM), hpg * HEAD_DIM), hpg * HEAD_DIM)
            v = (jnp.concatenate(yps, axis=1) + x_ref[:, lanes] * dsk_ref[:, lanes]) * _silu(z_ref[:, lanes])
            r = lax.rsqrt(jnp.mean(v * v, axis=-1, keepdims=True) + EPS)
            y_ref[:, lanes] = (v * r * nw_ref[:, lanes]).astype(y_ref.dtype)
        return carry

    lax.fori_loop(0, N_GROUPS, group, 0, unroll=True)

    @pl.when(step == pl.num_programs(0) - 1)
    def _():
        sf_ref[...] = st[...]


def ssd_scan_dir(xbc, dt_t, a_row, s0, y_prev, *, direction, gate=None):
    n = xbc.shape[0]
    nh = dt_t.shape[0] // 2
    d_inner = nh * HEAD_DIM
    gn = N_GROUPS * D_STATE
    assert n % CHUNK == 0 and d_inner % gn == 0
    nc = n // CHUNK
    rev = direction == 1
    cidx = (lambda i: nc - 1 - i) if rev else (lambda i: i)
    ins = [xbc, xbc, xbc, dt_t, a_row, s0]
    specs = [
        pl.BlockSpec((CHUNK, d_inner), lambda i: (cidx(i), 0)),
        pl.BlockSpec((CHUNK, gn), lambda i: (cidx(i), d_inner // gn)),
        pl.BlockSpec((CHUNK, gn), lambda i: (cidx(i), d_inner // gn + 1)),
        pl.BlockSpec((nh, CHUNK), lambda i: (direction, cidx(i))),
        pl.BlockSpec((nh, CHUNK), lambda i: (0, 0)),
        pl.BlockSpec(s0.shape, lambda i: (0, 0, 0)),
    ]
    if y_prev is not None:
        ins.append(y_prev)
        specs.append(pl.BlockSpec((CHUNK, d_inner), lambda i: (cidx(i), 0)))
    if gate is not None:
        zx, d_row, norm_w = gate
        vec = pl.BlockSpec((1, d_inner), lambda i: (0, 0))
        ins += [zx, d_row.reshape(1, d_inner), norm_w.reshape(1, d_inner)]
        specs += [pl.BlockSpec((CHUNK, d_inner), lambda i: (cidx(i), 0)), vec, vec]
    return pl.pallas_call(
        functools.partial(_scan_kernel, rev=rev, has_prev=y_prev is not None, gated=gate is not None),
        grid=(nc,),
        in_specs=specs,
        out_specs=[pl.BlockSpec((CHUNK, d_inner), lambda i: (cidx(i), 0)),
                   pl.BlockSpec(s0.shape, lambda i: (0, 0, 0))],
        out_shape=[jax.ShapeDtypeStruct((n, d_inner), F32 if gate is None else BF16),
                   jax.ShapeDtypeStruct(s0.shape, F32)],
        scratch_shapes=[pltpu.VMEM(s0.shape, F32), pltpu.VMEM((5, nh, CHUNK), F32)],
        compiler_params=_params("arbitrary"),
        name="scan_bwd" if rev else "scan_fwd",
    )(*ins)


def _row_tile(n, pref):
    return pref if n % pref == 0 else n


def ssd_layer(x, ctx_s, mod_l, mod_c, norm1, w_in, conv_w, conv_b, dt_bias, a_log, d_skip, norm_w, w_out, k, ctx_out):
    nh = a_log.shape[1]
    d_inner = nh * HEAD_DIM
    gn = N_GROUPS * D_STATE
    a_rows = jnp.broadcast_to(a_log.astype(F32)[:, :, None], (2, nh, CHUNK))
    d_row = jnp.repeat(d_skip.astype(F32), HEAD_DIM)
    in_tn = w_in.shape[-1] // 9

    def project(v, mod, width):
        n = v.shape[0]
        zx = matmul(v, w_in, w_index=k, mod=(norm1, mod[1], mod[0]), tm=_row_tile(n, 1024), tn=in_tn,
                    name="ssd_in")
        xbc = conv_silu(zx, conv_w, conv_b, width=width, col0=d_inner, tb=_row_tile(n, 1024))
        dt_t = dt_rows(zx, dt_bias, col0=2 * d_inner + 2 * gn, tb=_row_tile(n, 512))
        return zx, xbc, dt_t

    zx_c, xbc_c, dt_c = project(ctx_s, mod_c, ctx_s.shape[0])
    zx_l, xbc_l, dt_l = project(x, mod_l, GRID_W)
    zero = jnp.zeros((nh // 2, 2 * D_STATE, V7X_LANES), F32)
    yc, s_f = ssd_scan_dir(xbc_c, dt_c, a_rows[0], zero, None, direction=0)
    yl, _ = ssd_scan_dir(xbc_l, dt_l, a_rows[0], s_f, None, direction=0)
    yc, s_b = ssd_scan_dir(xbc_c, dt_c, a_rows[1], zero, yc, direction=1,
                           gate=(zx_c, d_row, norm_w) if ctx_out else None)
    yl, _ = ssd_scan_dir(xbc_l, dt_l, a_rows[1], s_b, yl, direction=1, gate=(zx_l, d_row, norm_w))

    def output(v, yn, gate):
        return matmul(yn, w_out, w_index=k, res=v, gate=gate, tm=_row_tile(v.shape[0], 512), tn=1024,
                      w_outer=True, name="ssd_out")

    x_new = output(x, yl, mod_l[2])
    ctx_new = output(ctx_s, yc, mod_c[2]) if ctx_out else None
    return x_new, ctx_new


def _dft_tables(n):
    j = np.arange(n, dtype=np.int64)
    ang = ((j[:, None] * j[None, :]) % n).astype(np.float64) * (2.0 * math.pi / n)
    return np.cos(ang), np.sin(ang)


def _fchan_kernel(x_ref, gain_ref, sc_ref, sh_ref, cs_ref, o_ref):
    h = _rms_mod(x_ref[...], gain_ref[...], sc_ref[...], sh_ref[...]).astype(BF16)
    gw = cs_ref.shape[0]
    d = h.shape[1]
    for g in range(d // gw):
        r = jnp.dot(h[:, g * gw:(g + 1) * gw], cs_ref[...], preferred_element_type=F32)
        o_ref[:, g * gw:(g + 1) * gw] = r[:, :gw]
        o_ref[:, d + g * gw:d + (g + 1) * gw] = r[:, gw:]


def _f1_kernel(r_ref, i_ref, k1_ref, tc_ref, ts_ref, yr_ref, yi_ref):
    la, sub, cw = r_ref.shape
    rows = la * sub
    xx = jnp.concatenate([r_ref[...].reshape(rows, cw), i_ref[...].reshape(rows, cw)], axis=0).astype(BF16)
    y = jnp.dot(k1_ref[...], xx, preferred_element_type=F32)
    yr, yi = y[:rows], y[rows:]
    c, s = tc_ref[...], ts_ref[...]
    yr_ref[...] = (c * yr + s * yi).reshape(la, sub, cw)
    yi_ref[...] = (c * yi - s * yr).reshape(la, sub, cw)


def _f2_kernel(r_ref, i_ref, k2_ref, o_ref):
    sub, lb, cw = r_ref.shape
    xx = jnp.concatenate([r_ref[...].reshape(sub * lb, cw), i_ref[...].reshape(sub * lb, cw)],
                         axis=0).astype(BF16)
    o_ref[...] = jnp.dot(k2_ref[...], xx, preferred_element_type=F32).reshape(lb, sub, cw)


def fnet_layer(x, mod, norm1, w_out, k, *, cw=512):
    l, d = x.shape
    sub = V7X_SUBLANES
    la = l // 128 if l >= 1024 else sub
    lb = l // la
    gw = d // FNET_GROUPS
    assert la % sub == 0 and lb % sub == 0 and d % cw == 0
    shift, scale, gate = mod
    cc, sc = _dft_tables(gw)
    norm = 1.0 / math.sqrt(l * gw)
    cs = jnp.asarray(np.concatenate([cc, -sc], axis=1) * norm, dtype=BF16)
    tm = _row_tile(l, 512)
    vec = pl.BlockSpec((1, d), lambda i: (0, 0))
    g2 = pl.pallas_call(
        _fchan_kernel,
        grid=(l // tm,),
        in_specs=[pl.BlockSpec((tm, d), lambda i: (i, 0)), vec, vec, vec, pl.BlockSpec((gw, 2 * gw), lambda i: (0, 0))],
        out_specs=pl.BlockSpec((tm, 2 * d), lambda i: (i, 0)),
        out_shape=jax.ShapeDtypeStruct((l, 2 * d), F32),
        compiler_params=_params("arbitrary"),
        name="fnet_chan",
    )(x, norm1.reshape(1, d), scale.reshape(1, d), shift.reshape(1, d), cs)
    ca, sa = _dft_tables(la)
    eye = np.eye(sub)
    k1 = jnp.asarray(np.block([[np.kron(ca, eye), np.kron(sa, eye)], [np.kron(-sa, eye), np.kron(ca, eye)]]),
                     dtype=BF16)
    bi = np.arange(lb, dtype=np.int64).reshape(lb // sub, 1, sub)
    di = np.arange(la, dtype=np.int64).reshape(1, la, 1)
    ang = ((bi * di) % l).astype(np.float64) * (2.0 * math.pi / l)
    tc = jnp.asarray(np.cos(ang).reshape(lb // sub, la * sub, 1), dtype=F32)
    ts = jnp.asarray(np.sin(ang).reshape(lb // sub, la * sub, 1), dtype=F32)
    g3 = g2.reshape(la, lb, 2 * d)
    ncb = d // cw
    blk1 = lambda off: pl.BlockSpec((la, sub, cw), lambda b, c: (0, b, off + c))
    twb = pl.BlockSpec((None, la * sub, 1), lambda b, c: (b, 0, 0))
    yr3, yi3 = pl.pallas_call(
        _f1_kernel,
        grid=(lb // sub, ncb),
        in_specs=[blk1(0), blk1(ncb), pl.BlockSpec(k1.shape, lambda b, c: (0, 0)), twb, twb],
        out_specs=[blk1(0), blk1(0)],
        out_shape=[jax.ShapeDtypeStruct((la, lb, d), F32)] * 2,
        compiler_params=_params("arbitrary", "arbitrary"),
        name="fnet_stage1",
    )(g3, g3, k1, tc, ts)
    cb_, sb_ = _dft_tables(lb)
    k2 = jnp.asarray(np.concatenate([np.einsum("cb,de->cdeb", cb_, eye).reshape(lb * sub, sub * lb),
                                     np.einsum("cb,de->cdeb", sb_, eye).reshape(lb * sub, sub * lb)], axis=1),
                     dtype=BF16)
    blk2 = pl.BlockSpec((sub, lb, cw), lambda dd, c: (dd, 0, c))
    f3 = pl.pallas_call(
        _f2_kernel,
        grid=(la // sub, ncb),
        in_specs=[blk2, blk2, pl.BlockSpec(k2.shape, lambda dd, c: (0, 0))],
        out_specs=pl.BlockSpec((lb, sub, cw), lambda dd, c: (0, dd, c)),
        out_shape=jax.ShapeDtypeStruct((lb, la, d), F32),
        compiler_params=_params("arbitrary", "arbitrary"),
        name="fnet_stage2",
    )(yr3, yi3, k2)
    return matmul(f3.reshape(l, d), w_out, w_index=k, res=x, gate=gate, tm=_row_tile(l, 512), tn=d,
                  name="fnet_out")


ROUTE_S = 128
ROUTE_TOKENS = ROUTE_S * ROUTE_S


def _moe_prep_kernel(x_ref, gain_ref, sc_ref, sh_ref, wr_ref, h_ref, lg_ref):
    h = _rms_mod(x_ref[...], gain_ref[...], sc_ref[...], sh_ref[...])
    tb = h.shape[0]
    for j in range(TOK_ROWS):
        h_ref[pl.ds(j, tb, stride=TOK_ROWS), :] = h[:, j * V7X_LANES:(j + 1) * V7X_LANES]
    def split(v):
        hi = v.astype(BF16)
        return hi, (v - hi.astype(F32)).astype(BF16)

    nt = lambda a, b: lax.dot_general(a, b, (((1,), (1,)), ((), ())), preferred_element_type=F32)
    (w_hi, w_lo), (h_hi, h_lo) = split(wr_ref[...]), split(h)
    lg_ref[...] = nt(w_hi, h_hi) + (nt(w_lo, h_hi) + nt(w_hi, h_lo))


def moe_prep(x, norm2, scale, shift, w_router, *, tb):
    n, d = x.shape
    e = w_router.shape[1]
    assert d == TOK_ROWS * V7X_LANES and n % tb == 0
    vec = pl.BlockSpec((1, d), lambda i: (0, 0))
    return pl.pallas_call(
        _moe_prep_kernel,
        grid=(n // tb,),
        in_specs=[pl.BlockSpec((tb, d), lambda i: (i, 0)), vec, vec, vec, pl.BlockSpec((e, d), lambda i: (0, 0))],
        out_specs=[pl.BlockSpec((tb * TOK_ROWS, V7X_LANES), lambda i: (i, 0)), pl.BlockSpec((e, tb), lambda i: (0, i))],
        out_shape=[jax.ShapeDtypeStruct((n * TOK_ROWS, V7X_LANES), F32), jax.ShapeDtypeStruct((e, n), F32)],
        compiler_params=_params("arbitrary"),
        name="moe_prep",
    )(x, norm2.reshape(1, d), scale.reshape(1, d), shift.reshape(1, d), w_router.T)


def _tok_cumsum(m, lincl_bf16, ustrict):
    wc = jnp.dot(lincl_bf16, m.astype(BF16), preferred_element_type=F32)
    coltot = wc[ROUTE_S - 1:ROUTE_S, :]
    colpref = jnp.dot(jnp.broadcast_to(coltot, (V7X_SUBLANES, ROUTE_S)), ustrict, precision=HIGHEST,
                      preferred_element_type=F32)[0:1]
    return wc, coltot, colpref


def _route_kernel(lg_ref, idx_ref, q_ref, g_ref, off_ref, cnt_ref, aff_scr, bits_scr, sel_scr, qt_scr, *,
                  n_valid, cap, pc):
    ne = lg_ref.shape[0]
    s_ = ROUTE_S
    c_pad = idx_ref.shape[2]
    si = lax.broadcasted_iota(I32, (s_, s_), 0)
    ji = lax.broadcasted_iota(I32, (s_, s_), 1)
    valid = ji * s_ + si < n_valid
    lincl = (ji <= si).astype(BF16)
    ustrict = (si < ji).astype(F32)
    ones = jnp.ones((s_, s_), BF16)

    ls = [lg_ref[e] for e in range(ne)]
    mx = functools.reduce(jnp.maximum, ls)
    ex = [jnp.exp(l - mx) for l in ls]
    den = functools.reduce(jnp.add, ex)
    for e in range(ne):
        aff = jnp.where(valid, ex[e] / den, -1.0)
        aff_scr[e] = aff
        bits_scr[e] = pltpu.bitcast(aff, I32)

    def bit_step(i, ts):
        bit = jnp.left_shift(jnp.int32(1), 30 - i)
        out = []
        for e in range(ne):
            cand = ts[e] | bit
            ge = (bits_scr[e] >= cand).astype(F32)
            c1 = jnp.sum(jnp.sum(ge, axis=0, keepdims=True), axis=1, keepdims=True)
            out.append(jnp.where(c1 >= cap, cand, ts[e]))
        return tuple(out)

    ts = lax.fori_loop(0, 31, bit_step, tuple(jnp.zeros((1, 1), I32) for _ in range(ne)))

    cnt = jnp.zeros((s_, s_), F32)
    for e in range(ne):
        bits = bits_scr[e]
        gt = bits > ts[e]
        eq = (bits == ts[e]).astype(F32)
        n_gt = jnp.sum(jnp.sum(gt.astype(F32), axis=0, keepdims=True), axis=1, keepdims=True)
        wc, _, colpref = _tok_cumsum(eq, lincl, ustrict)
        rank_eq = wc + colpref - eq
        sel = jnp.where(gt | ((eq > 0.0) & (rank_eq < cap - n_gt)), 1.0, 0.0)
        sel_scr[e] = sel
        qt_scr[e] = cnt
        cnt = cnt + sel
    wc, _, colpref = _tok_cumsum(cnt, lincl, ustrict)
    offs = wc + colpref - cnt
    off_ref[...] = offs
    cnt_ref[...] = cnt

    def per_expert(e, carry):
        m = sel_scr[e]
        qt = qt_scr[e] + offs
        aff = aff_scr[e]
        wc, coltot, colpref = _tok_cumsum(m, lincl, ustrict)
        wc_b = wc.astype(BF16)
        colcum = jnp.dot((lincl.astype(F32) * coltot).astype(BF16), ones, preferred_element_type=F32)
        colpref8 = jnp.broadcast_to(colpref, (V7X_SUBLANES, s_))
        sub = lax.broadcasted_iota(I32, (s_, pc), 0).astype(F32)
        reps = pc // s_
        colcum_t = jnp.concatenate([colcum] * reps, axis=1) if reps > 1 else colcum
        for c0 in range(0, c_pad, pc):
            p = (lax.broadcasted_iota(I32, (s_, pc), 1) + c0).astype(F32)
            p_row = p[0:1]
            blk = jnp.sum((colcum_t <= p).astype(F32), axis=0, keepdims=True)
            oh_j = (sub == blk).astype(F32)
            colvec = jnp.dot(wc_b, oh_j.astype(BF16), preferred_element_type=F32)
            cp = jnp.dot(colpref8, oh_j, precision=HIGHEST, preferred_element_type=F32)[0:1]
            s_idx = jnp.sum((colvec <= p_row - cp).astype(F32), axis=0, keepdims=True)
            oh_s = (sub == s_idx).astype(F32)
            gv = jnp.sum(oh_s * jnp.dot(aff, oh_j, precision=HIGHEST, preferred_element_type=F32),
                         axis=0, keepdims=True)
            qv = jnp.sum(oh_s * jnp.dot(qt, oh_j, precision=HIGHEST, preferred_element_type=F32),
                         axis=0, keepdims=True)
            live = p_row < cap
            idx_ref[e, :, c0:c0 + pc] = jnp.where(live, blk * s_ + s_idx, 0.0).astype(I32)
            q_ref[e, :, c0:c0 + pc] = jnp.where(live, qv, 0.0).astype(I32)
            g_ref[e, :, c0:c0 + pc] = jnp.where(live, gv, 0.0)
        return carry

    lax.fori_loop(0, ne, per_expert, 0)


def moe_route(logits_t, *, cap):
    ne, n = logits_t.shape
    assert n <= ROUTE_TOKENS and cap <= n
    s_ = ROUTE_S
    c_pad = max(cap, s_)
    pc = min(c_pad, 2 * s_)
    assert c_pad % pc == 0
    lg = jnp.pad(logits_t, ((0, 0), (0, ROUTE_TOKENS - n))).reshape(ne, s_, s_).transpose(0, 2, 1)
    full = lambda shape: pl.BlockSpec(shape, lambda i: (0,) * len(shape))
    slot = jax.ShapeDtypeStruct((ne, 1, c_pad), I32)
    idx, q, g, offs, cnt = pl.pallas_call(
        functools.partial(_route_kernel, n_valid=n, cap=cap, pc=pc),
        grid=(1,),
        in_specs=[full((ne, s_, s_))],
        out_specs=[full((ne, 1, c_pad))] * 3 + [full((s_, s_))] * 2,
        out_shape=[slot, slot, jax.ShapeDtypeStruct((ne, 1, c_pad), F32),
                   jax.ShapeDtypeStruct((s_, s_), F32), jax.ShapeDtypeStruct((s_, s_), F32)],
        scratch_shapes=[pltpu.VMEM((ne, s_, s_), F32), pltpu.VMEM((ne, s_, s_), I32),
                        pltpu.VMEM((ne, s_, s_), F32), pltpu.VMEM((ne, s_, s_), F32)],
        compiler_params=_params("arbitrary"),
        name="moe_route",
    )(lg)
    tok = lambda a: a.T.reshape(-1)[:n]
    return idx[:, 0, :cap], q[:, 0, :cap], g[:, 0, :cap], tok(offs), tok(cnt)


def _ffn_kernel(idx_ref, q_ref, g_ref, h_hbm, wg_ref, wu_ref, wd_ref, z_hbm, ga, gb, sa, sb, sems):
    step = pl.program_id(0)
    last = pl.num_programs(0) - 1
    cbh = ga.shape[0] // TOK_ROWS
    total = idx_ref.shape[0]
    base = step * (2 * cbh)
    srows = lambda p: pl.ds(p * TOK_ROWS, TOK_ROWS)
    drows = lambda r: pl.ds(pl.multiple_of(r * TOK_ROWS, TOK_ROWS), TOK_ROWS)
    G_A, G_B, S_A, S_B = range(4)

    def gather_start(buf, sem, slot0):
        for p in range(cbh):
            pltpu.make_async_copy(h_hbm.at[drows(idx_ref[slot0 + p]), :], buf.at[srows(p), :],
                                  sems.at[sem]).start(priority=p % 2)

    def gather_wait(buf, sem):
        for p in range(cbh):
            pltpu.make_async_copy(h_hbm.at[srows(0), :], buf.at[srows(p), :], sems.at[sem]).wait()

    def scatter_start(buf, sem, slot0):
        for p in range(cbh):
            pltpu.make_async_copy(buf.at[srows(p), :], z_hbm.at[drows(q_ref[slot0 + p]), :],
                                  sems.at[sem]).start(priority=p % 2)

    def scatter_wait(buf, sem):
        for p in range(cbh):
            pltpu.make_async_copy(buf.at[srows(p), :], z_hbm.at[srows(0), :], sems.at[sem]).wait()

    def compute(gbuf, sbuf, g):
        x = jnp.concatenate([gbuf[pl.ds(j, cbh, stride=TOK_ROWS), :] for j in range(TOK_ROWS)],
                            axis=1).astype(BF16)
        a = jnp.dot(x, wg_ref[...], preferred_element_type=F32)
        u = jnp.dot(x, wu_ref[...], preferred_element_type=F32)
        y = jnp.dot((_silu(a) * u).astype(BF16), wd_ref[...], preferred_element_type=F32) * g
        for j in range(TOK_ROWS):
            sbuf[pl.ds(j, cbh, stride=TOK_ROWS), :] = y[:, j * V7X_LANES:(j + 1) * V7X_LANES]

    @pl.when(step == 0)
    def _():
        gather_start(ga, G_A, 0)

    gather_wait(ga, G_A)

    @pl.when(step > 0)
    def _():
        scatter_wait(sa, S_A)

    gather_start(gb, G_B, base + cbh)
    compute(ga, sa, g_ref[0:cbh, :])
    scatter_start(sa, S_A, base)
    gather_wait(gb, G_B)

    @pl.when(step > 0)
    def _():
        scatter_wait(sb, S_B)

    gather_start(ga, G_A, jnp.minimum(base + 2 * cbh, total - cbh))
    compute(gb, sb, g_ref[cbh:2 * cbh, :])
    scatter_start(sb, S_B, base + cbh)

    @pl.when(step == last)
    def _():
        gather_wait(ga, G_A)
        scatter_wait(sa, S_A)
        scatter_wait(sb, S_B)


def moe_ffn(h_rows, idx, q, g, w_gate, w_up, w_down, layer, *, cbh):
    ne, cap = idx.shape
    d, f = w_gate.shape[2:]
    sb_rows = 2 * cbh
    assert cap % sb_rows == 0
    spe = cap // sb_rows
    wmap = lambda s, *_: (layer, s // spe, 0, 0)
    buf = pltpu.VMEM((cbh * TOK_ROWS, V7X_LANES), F32)
    return pl.pallas_call(
        _ffn_kernel,
        grid_spec=pltpu.PrefetchScalarGridSpec(
            num_scalar_prefetch=2,
            grid=(ne * spe,),
            in_specs=[
                pl.BlockSpec((sb_rows, 1), lambda s, *_: (s, 0)),
                pl.BlockSpec(memory_space=pl.ANY),
                pl.BlockSpec((None, None, d, f), wmap),
                pl.BlockSpec((None, None, d, f), wmap),
                pl.BlockSpec((None, None, f, d), wmap),
            ],
            out_specs=pl.BlockSpec(memory_space=pl.ANY),
            scratch_shapes=[buf, buf, buf, buf, pltpu.SemaphoreType.DMA((4,))],
        ),
        out_shape=jax.ShapeDtypeStruct((ne * cap * TOK_ROWS, V7X_LANES), F32),
        compiler_params=_params("arbitrary"),
        name="moe_ffn",
    )(idx.reshape(-1), q.reshape(-1), g.reshape(ne * cap, 1), h_rows, w_gate, w_up, w_down)


def _combine_kernel(boff_ref, x_ref, gate_ref, off_ref, cnt_ref, z_hbm, *rest, ch, p_total, final):
    fw_ref = rest[0] if final else None
    o_ref, z0, z1, sems = rest[1:] if final else rest
    i = pl.program_id(0)
    start = boff_ref[i]
    end = boff_ref[i + 1]
    nch = (end - start + ch - 1) // ch
    lo = off_ref[...]
    hi = lo + cnt_ref[...]
    o_ref[...] = jnp.zeros(o_ref.shape, F32)
    zbufs = (z0, z1)

    def row_start(c, first_row=None):
        first_row = start if first_row is None else first_row
        return jnp.minimum(first_row + c * ch, p_total - ch)

    def copy(c, slot, first_row=None):
        src = z_hbm.at[pl.ds(pl.multiple_of(row_start(c, first_row) * TOK_ROWS, TOK_ROWS), ch * TOK_ROWS), :]
        return pltpu.make_async_copy(src, zbufs[slot], sems.at[slot])

    def process(c, slot):
        copy(c, slot).wait()

        @pl.when(c + 1 < nch)
        def _():
            copy(c + 1, 1 - slot).start()

        first = start + c * ch
        qabs = row_start(c) + lax.broadcasted_iota(I32, (1, ch), 1)
        qf = qabs.astype(F32)
        seg = ((qf >= lo) & (qf < hi) & (qabs >= first)).astype(BF16)
        zb = zbufs[slot]
        z = jnp.concatenate([zb[pl.ds(j, ch, stride=TOK_ROWS), :] for j in range(TOK_ROWS)], axis=1)
        o_ref[...] += jnp.dot(seg, z.astype(BF16), preferred_element_type=F32)

    @pl.when((i == 0) & (nch > 0))
    def _():
        copy(0, 0).start()

    def pair(k, carry):
        process(2 * k, 0)

        @pl.when(2 * k + 1 < nch)
        def _():
            process(2 * k + 1, 1)

        return carry

    lax.fori_loop(0, (nch + 1) // 2, pair, 0)

    @pl.when(i + 1 < pl.num_programs(0))
    def _():
        @pl.when(boff_ref[i + 2] > end)
        def _():
            copy(0, 0, first_row=end).start()

    out = x_ref[...] + gate_ref[...] * o_ref[...]
    if final:
        out = out * lax.rsqrt(jnp.mean(out * out, axis=-1, keepdims=True) + EPS) * fw_ref[...]
    o_ref[...] = out


def moe_combine(x, gate, z_rows, offs, cnt, *, tb, ch=256, final_w=None):
    n, d = x.shape
    p_total = z_rows.shape[0] // TOK_ROWS
    assert n % tb == 0 and p_total >= ch
    boff = jnp.concatenate([offs[::tb], jnp.full((1,), p_total, F32)]).astype(I32)
    col = pl.BlockSpec((tb, 1), lambda i, s: (i, 0))
    vec = pl.BlockSpec((1, d), lambda i, s: (0, 0))
    extra_in, extra_spec = ([final_w.reshape(1, d)], [vec]) if final_w is not None else ([], [])
    return pl.pallas_call(
        functools.partial(_combine_kernel, ch=ch, p_total=p_total, final=final_w is not None),
        grid_spec=pltpu.PrefetchScalarGridSpec(
            num_scalar_prefetch=1,
            grid=(n // tb,),
            in_specs=[pl.BlockSpec((tb, d), lambda i, s: (i, 0)), vec,
                      col, col, pl.BlockSpec(memory_space=pl.ANY)] + extra_spec,
            out_specs=pl.BlockSpec((tb, d), lambda i, s: (i, 0)),
            scratch_shapes=[pltpu.VMEM((ch * TOK_ROWS, V7X_LANES), F32), pltpu.VMEM((ch * TOK_ROWS, V7X_LANES), F32),
                            pltpu.SemaphoreType.DMA((2,))],
        ),
        out_shape=jax.ShapeDtypeStruct((n, d), F32),
        compiler_params=_params("arbitrary"),
        name="moe_combine",
    )(boff, x, gate.reshape(1, d), offs.reshape(n, 1), cnt.reshape(n, 1), z_rows, *extra_in)


def moe_layer(x, mod, norm2, w_router, w_gate, w_up, w_down, layer, final_w=None):
    n = x.shape[0]
    ne = w_router.shape[1]
    cap = (CAPACITY_FACTOR * n) // ne
    shift, scale, gate = mod
    h_rows, logits_t = moe_prep(x, norm2, scale, shift, w_router, tb=_row_tile(n, 256))
    idx, q, g, offs, cnt = moe_route(logits_t, cap=cap)
    z_rows = moe_ffn(h_rows, idx, q, g, w_gate, w_up, w_down, layer, cbh=min(cap // 2, 256))
    return moe_combine(x, gate, z_rows, offs, cnt, tb=_row_tile(n, 512), final_w=final_w)


def kernel(x, c, ctx, c_ctx, ada_w, ada_b, norm1_w, norm2_w, final_norm_w, ssd_w_in, ssd_conv_w, ssd_conv_b,
           ssd_dt_bias, ssd_a_log, ssd_d, ssd_norm_w, ssd_w_out, fnet_w_out, moe_w_router, moe_w_gate, moe_w_up,
           moe_w_down):
    depth = ada_w.shape[0]
    d = x.shape[-1]
    assert x.shape[0] == 1, "one sample per call"
    xs, cs = x[0], ctx[0]
    ada = ada_all(c, c_ctx, ada_w, ada_b)
    moe_w = (moe_w_gate.astype(BF16), moe_w_up.astype(BF16), moe_w_down.astype(BF16))
    w_in_b, w_out_b, w_fnet_b = ssd_w_in.astype(BF16), ssd_w_out.astype(BF16), fnet_w_out.astype(BF16)
    for i in range(depth):
        k = i // 2
        is_ssd = i % 2 == 0
        ctx_later = any(j % 2 == 0 for j in range(i + 1, depth))
        sh1, sc1, g1, sh2, sc2, g2 = (ada[i, 0, m * d:(m + 1) * d] for m in range(6))
        csh1, csc1, cg1, csh2, csc2, cg2 = (ada[i, 1, m * d:(m + 1) * d] for m in range(6))
        if is_ssd:
            xs, cs_new = ssd_layer(xs, cs, (sh1, sc1, g1), (csh1, csc1, cg1), norm1_w[i], w_in_b, ssd_conv_w[k],
                                   ssd_conv_b[k], ssd_dt_bias[k], ssd_a_log[k], ssd_d[k], ssd_norm_w[k],
                                   w_out_b, k, ctx_later)
        else:
            xs = fnet_layer(xs, (sh1, sc1, g1), norm1_w[i], w_fnet_b, k)
            cs_new = fnet_layer(cs, (csh1, csc1, cg1), norm1_w[i], w_fnet_b, k) if ctx_later else None
        xs = moe_layer(xs, (sh2, sc2, g2), norm2_w[i], moe_w_router[i], *moe_w, i,
                       final_w=final_norm_w if i == depth - 1 else None)
        if ctx_later:
            cs = moe_layer(cs_new, (csh2, csc2, cg2), norm2_w[i], moe_w_router[i], *moe_w, i)
    return xs[None]
```

```python
import functools
import math

import jax
import jax.numpy as jnp
import numpy as np
from jax import lax
from jax.experimental import pallas as pl
from jax.experimental.pallas import tpu as pltpu

F32 = jnp.float32
BF16 = jnp.bfloat16
I32 = jnp.int32
HIGHEST = lax.Precision.HIGHEST

V7X_LANES = 128
V7X_SUBLANES = 8
V7X_VMEM_BYTES = 64 * 1024 * 1024
VMEM_LIMIT = V7X_VMEM_BYTES - 8 * 1024 * 1024

GRID_W = 64
HEAD_DIM = 64
N_GROUPS = 8
D_STATE = 128
CHUNK = 128
FNET_GROUPS = 8
N_EXPERTS = 16
CAPACITY_FACTOR = 2
EPS = 1e-6
TOK_ROWS = 16


def _params(*sem):
    return pltpu.CompilerParams(dimension_semantics=sem, vmem_limit_bytes=VMEM_LIMIT)


def _rms_mod(x, gain, scale, shift):
    y = x * lax.rsqrt(jnp.mean(x * x, axis=-1, keepdims=True) + EPS)
    return y * gain * (1.0 + scale) + shift


def _silu(x):
    return x * jax.nn.sigmoid(x)


def _ada_kernel(c_ref, w_ref, b_ref, o_ref):
    w = w_ref[...]
    rows = []
    for r in range(2):
        s = _silu(c_ref[r])
        rows.append(jnp.sum(s * w, axis=0, keepdims=True) + b_ref[...])
    rows.append(jnp.zeros((V7X_SUBLANES - 2, w.shape[1]), F32))
    o_ref[...] = jnp.concatenate(rows, axis=0)


def ada_all(c, c_ctx, ada_w, ada_b):
    depth, k, n6 = ada_w.shape
    tn = n6 // 8
    cond = jnp.stack([c[0], c_ctx], axis=0)[:, :, None]
    out = pl.pallas_call(
        _ada_kernel,
        grid=(depth, n6 // tn),
        in_specs=[
            pl.BlockSpec((2, k, 1), lambda l, n: (0, 0, 0)),
            pl.BlockSpec((None, k, tn), lambda l, n: (l, 0, n)),
            pl.BlockSpec((None, 1, tn), lambda l, n: (l, 0, n)),
        ],
        out_specs=pl.BlockSpec((None, V7X_SUBLANES, tn), lambda l, n: (l, 0, n)),
        out_shape=jax.ShapeDtypeStruct((depth, V7X_SUBLANES, n6), F32),
        compiler_params=_params("arbitrary", "arbitrary"),
        name="ada",
    )(cond, ada_w, ada_b[:, None, :])
    return out


def _mm_kernel(*refs, prologue, epilogue):
    it = iter(refs)
    a_ref = next(it)
    if prologue:
        gain_ref, sc_ref, sh_ref = next(it), next(it), next(it)
    w_ref = next(it)
    if epilogue:
        res_ref, gate_ref = next(it), next(it)
    o_ref = next(it)
    a_scr = next(it, None)

    if a_scr is None:
        a_bf16 = a_ref[...]
    else:
        @pl.when(pl.program_id(1) == 0)
        def _():
            a = a_ref[...]
            if prologue:
                a = _rms_mod(a, gain_ref[...], sc_ref[...], sh_ref[...])
            a_scr[...] = a.astype(BF16)

        a_bf16 = a_scr[...]
    acc = jnp.dot(a_bf16, w_ref[...], preferred_element_type=F32)
    if epilogue:
        acc = res_ref[...] + gate_ref[...] * acc
    o_ref[...] = acc.astype(o_ref.dtype)


def matmul(a, w_bf16, *, w_index=None, mod=None, res=None, gate=None, tm, tn, out_dtype=F32, name="mm",
           w_outer=False):
    m, k = a.shape
    n = w_bf16.shape[-1]
    assert m % tm == 0 and n % tn == 0
    use_scr = mod is not None or a.dtype != BF16
    assert not (w_outer and use_scr)
    ij = (lambda g0, g1: (g1, g0)) if w_outer else (lambda g0, g1: (g0, g1))
    row = lambda g0, g1: (0, 0)
    ins, specs = [a], [pl.BlockSpec((tm, k), lambda g0, g1: (ij(g0, g1)[0], 0))]
    if mod is not None:
        for v in mod:
            ins.append(v.reshape(1, k))
            specs.append(pl.BlockSpec((1, k), row))
    ins.append(w_bf16)
    if w_index is None:
        specs.append(pl.BlockSpec((k, tn), lambda g0, g1: (0, ij(g0, g1)[1])))
    else:
        specs.append(pl.BlockSpec((None, k, tn), lambda g0, g1: (w_index, 0, ij(g0, g1)[1])))
    if res is not None:
        ins += [res, gate.reshape(1, n)]
        specs += [pl.BlockSpec((tm, tn), ij), pl.BlockSpec((1, tn), lambda g0, g1: (0, ij(g0, g1)[1]))]
    return pl.pallas_call(
        functools.partial(_mm_kernel, prologue=mod is not None, epilogue=res is not None),
        grid=(n // tn, m // tm) if w_outer else (m // tm, n // tn),
        in_specs=specs,
        out_specs=pl.BlockSpec((tm, tn), ij),
        out_shape=jax.ShapeDtypeStruct((m, n), out_dtype),
        scratch_shapes=[pltpu.VMEM((tm, k), BF16)] if use_scr else [],
        compiler_params=_params("arbitrary", "arbitrary"),
        name=name,
    )(*ins)


def _conv_kernel(prev_ref, cur_ref, next_ref, w_ref, b_ref, o_ref, *, width):
    i = pl.program_id(0)
    last = pl.num_programs(0) - 1
    cur = cur_ref[...].astype(F32)
    tb, cb = cur.shape
    prev = jnp.where(i > 0, prev_ref[...].astype(F32), 0.0)
    nxt = jnp.where(i < last, next_ref[...].astype(F32), 0.0)
    if tb > width:
        up = jnp.concatenate([prev, cur[: tb - width]], axis=0)
        dn = jnp.concatenate([cur[width:], nxt], axis=0)
    else:
        up, dn = prev, nxt
    w = w_ref[...]

    def tap(dw):
        return up * w[dw:dw + 1] + cur * w[3 + dw:4 + dw] + dn * w[6 + dw:7 + dw]

    col = lax.broadcasted_iota(I32, (tb, cb), 0) & (width - 1)
    left = jnp.where(col == 0, 0.0, pltpu.roll(tap(0), 1, axis=0))
    right = jnp.where(col == width - 1, 0.0, pltpu.roll(tap(2), tb - 1, axis=0))
    o_ref[...] = _silu(left + tap(1) + right + b_ref[...])


def conv_silu(zx, conv_w, conv_b, *, width, col0, tb, cb=512):
    n = zx.shape[0]
    c = conv_w.shape[-1]
    assert width & (width - 1) == 0 and tb % width == 0 and n % tb == 0 and c % cb == 0 and col0 % cb == 0
    cblk0 = col0 // cb
    per = tb // width
    nrow = n // width
    return pl.pallas_call(
        functools.partial(_conv_kernel, width=width),
        grid=(n // tb, c // cb),
        in_specs=[
            pl.BlockSpec((width, cb), lambda i, j: (jnp.maximum(i * per - 1, 0), cblk0 + j)),
            pl.BlockSpec((tb, cb), lambda i, j: (i, cblk0 + j)),
            pl.BlockSpec((width, cb), lambda i, j: (jnp.minimum((i + 1) * per, nrow - 1), cblk0 + j)),
            pl.BlockSpec((9, cb), lambda i, j: (0, j)),
            pl.BlockSpec((1, cb), lambda i, j: (0, j)),
        ],
        out_specs=pl.BlockSpec((tb, cb), lambda i, j: (i, j)),
        out_shape=jax.ShapeDtypeStruct((n, c), F32),
        compiler_params=_params("arbitrary", "arbitrary"),
        name="conv",
    )(zx, zx, zx, conv_w.reshape(9, c), conv_b.reshape(1, c))


def _dt_kernel(x_ref, b_ref, o_ref):
    v = x_ref[...].astype(F32) + b_ref[...]
    sp = jnp.maximum(v, 0.0) + jnp.log1p(jnp.exp(-jnp.abs(v)))
    o_ref[...] = sp.T


def dt_rows(zx, dt_bias, *, col0, tb):
    n = zx.shape[0]
    w = dt_bias.size
    assert col0 % w == 0 and n % tb == 0
    return pl.pallas_call(
        _dt_kernel,
        grid=(n // tb,),
        in_specs=[pl.BlockSpec((tb, w), lambda i: (i, col0 // w)), pl.BlockSpec((1, w), lambda i: (0, 0))],
        out_specs=pl.BlockSpec((w, tb), lambda i: (0, i)),
        out_shape=jax.ShapeDtypeStruct((w, n), F32),
        compiler_params=_params("arbitrary"),
        name="dt",
    )(zx, dt_bias.reshape(1, w))


def _scan_kernel(x_ref, b_ref, c_ref, dt_ref, a_ref, s0_ref, *rest, rev, has_prev, gated):
    rest = list(rest)
    yp_ref = rest.pop(0) if has_prev else None
    z_ref, dsk_ref, nw_ref = (rest.pop(0), rest.pop(0), rest.pop(0)) if gated else (None, None, None)
    y_ref, sf_ref, st, rows = rest
    step = pl.program_id(0)
    q = CHUNK
    nh = dt_ref.shape[0]
    hpg = nh // N_GROUPS
    half = V7X_LANES // 2

    @pl.when(step == 0)
    def _():
        st[...] = s0_ref[...]

    ri = lax.broadcasted_iota(I32, (q, q), 0)
    ci = lax.broadcasted_iota(I32, (q, q), 1)
    if rev:
        upper, keep = ri >= ci, ri <= ci
    else:
        upper, keep = ri <= ci, ri >= ci
    lm = keep.astype(F32)
    dt = dt_ref[...]
    da = dt * -jnp.exp(a_ref[...])
    acs = jnp.dot(da, upper.astype(F32), precision=HIGHEST, preferred_element_type=F32)
    tot = jnp.sum(da, axis=1, keepdims=True)
    rows[0] = dt
    rows[1] = da
    rows[2] = acs
    rows[3] = dt * jnp.exp(tot - acs)
    rows[4] = jnp.broadcast_to(jnp.exp(tot), (nh, q))

    lane_lo = lax.broadcasted_iota(I32, (q, V7X_LANES), 1) < half
    r2 = lax.broadcasted_iota(I32, (2 * D_STATE, V7X_LANES), 0) < D_STATE
    l2 = lax.broadcasted_iota(I32, (2 * D_STATE, V7X_LANES), 1) < half
    diag = r2 == l2

    def group(g, carry):
        r0 = pl.multiple_of(g * hpg, hpg)
        dtg = rows[0, pl.ds(r0, hpg), :]
        dag = rows[1, pl.ds(r0, hpg), :]
        acg = rows[2, pl.ds(r0, hpg), :]
        wtg = rows[3, pl.ds(r0, hpg), :]
        etg = rows[4, pl.ds(r0, hpg), :]
        c0 = pl.multiple_of(g * D_STATE, D_STATE)
        cg = c_ref[:, pl.ds(c0, D_STATE)]
        bg = b_ref[:, pl.ds(c0, D_STATE)]
        cb = lax.dot_general(cg.astype(BF16), bg.astype(BF16), (((1,), (1,)), ((), ())),
                             preferred_element_type=F32)
        bgt = bg.T
        yps = []
        for k in range(hpg // 2):
            x0 = pl.multiple_of(g * (hpg * HEAD_DIM) + k * V7X_LANES, V7X_LANES)
            x2 = x_ref[:, pl.ds(x0, V7X_LANES)]
            ms, ces, bws, ess = [], [], [], []
            for t in range(2):
                j = 2 * k + t
                acol = jnp.sum(lm * dag[j:j + 1, :], axis=1, keepdims=True)
                dec = jnp.where(keep, jnp.exp(acol - acg[j:j + 1, :]), 0.0)
                ms.append(cb * dec * dtg[j:j + 1, :])
                ces.append(cg * jnp.exp(acol))
                bws.append(bgt * wtg[j:j + 1, :])
                ess.append(jnp.broadcast_to(etg[j:j + 1, :], (D_STATE, V7X_LANES)))
            sp = st[g * (hpg // 2) + k]
            lhs = jnp.concatenate(ms + ces, axis=1).astype(BF16)
            rhs = jnp.concatenate([jnp.where(lane_lo, x2, 0.0), jnp.where(lane_lo, 0.0, x2), sp],
                                  axis=0).astype(BF16)
            yp = jnp.dot(lhs, rhs, preferred_element_type=F32)
            if has_prev:
                yp = yp + yp_ref[:, pl.ds(x0, V7X_LANES)]
            if gated:
                yps.append(yp)
            else:
                y_ref[:, pl.ds(x0, V7X_LANES)] = yp
            upd = jnp.dot(jnp.concatenate(bws, axis=0).astype(BF16), x2.astype(BF16),
                          preferred_element_type=F32)
            st[g * (hpg // 2) + k] = sp * jnp.concatenate(ess, axis=0) + jnp.where(diag, upd, 0.0)
        if gated:
            lanes = pl.ds(pl.multiple_of(g * (hpg * HEAD_DIM), hpg * HEAD_DIM), hpg * HEAD_DIM)
            v = ((jnp.concatenate(yps, axis=1) + x_ref[:, lanes] * dsk_ref[:, lanes])
                 * _silu(z_ref[:, lanes].astype(F32)))
            r = lax.rsqrt(jnp.mean(v * v, axis=-1, keepdims=True) + EPS)
            y_ref[:, lanes] = (v * r * nw_ref[:, lanes]).astype(y_ref.dtype)
        return carry

    lax.fori_loop(0, N_GROUPS, group, 0, unroll=True)

    @pl.when(step == pl.num_programs(0) - 1)
    def _():
        sf_ref[...] = st[...]


def ssd_scan_dir(xbc, dt_t, a_row, s0, y_prev, *, direction, gate=None):
    n = xbc.shape[0]
    nh = dt_t.shape[0] // 2
    d_inner = nh * HEAD_DIM
    gn = N_GROUPS * D_STATE
    assert n % CHUNK == 0 and d_inner % gn == 0
    nc = n // CHUNK
    rev = direction == 1
    cidx = (lambda i: nc - 1 - i) if rev else (lambda i: i)
    ins = [xbc, xbc, xbc, dt_t, a_row, s0]
    specs = [
        pl.BlockSpec((CHUNK, d_inner), lambda i: (cidx(i), 0)),
        pl.BlockSpec((CHUNK, gn), lambda i: (cidx(i), d_inner // gn)),
        pl.BlockSpec((CHUNK, gn), lambda i: (cidx(i), d_inner // gn + 1)),
        pl.BlockSpec((nh, CHUNK), lambda i: (direction, cidx(i))),
        pl.BlockSpec((nh, CHUNK), lambda i: (0, 0)),
        pl.BlockSpec(s0.shape, lambda i: (0, 0, 0)),
    ]
    if y_prev is not None:
        ins.append(y_prev)
        specs.append(pl.BlockSpec((CHUNK, d_inner), lambda i: (cidx(i), 0)))
    if gate is not None:
        zx, d_row, norm_w = gate
        vec = pl.BlockSpec((1, d_inner), lambda i: (0, 0))
        ins += [zx, d_row.reshape(1, d_inner), norm_w.reshape(1, d_inner)]
        specs += [pl.BlockSpec((CHUNK, d_inner), lambda i: (cidx(i), 0)), vec, vec]
    return pl.pallas_call(
        functools.partial(_scan_kernel, rev=rev, has_prev=y_prev is not None, gated=gate is not None),
        grid=(nc,),
        in_specs=specs,
        out_specs=[pl.BlockSpec((CHUNK, d_inner), lambda i: (cidx(i), 0)),
                   pl.BlockSpec(s0.shape, lambda i: (0, 0, 0))],
        out_shape=[jax.ShapeDtypeStruct((n, d_inner), F32 if gate is None else BF16),
                   jax.ShapeDtypeStruct(s0.shape, F32)],
        scratch_shapes=[pltpu.VMEM(s0.shape, F32), pltpu.VMEM((5, nh, CHUNK), F32)],
        compiler_params=_params("arbitrary"),
        name="scan_bwd" if rev else "scan_fwd",
    )(*ins)


def _row_tile(n, pref):
    return pref if n % pref == 0 else n


def ssd_layer(x, ctx_s, mod_l, mod_c, norm1, w_in, conv_w, conv_b, dt_bias, a_log, d_skip, norm_w, w_out, k, ctx_out):
    nh = a_log.shape[1]
    d_inner = nh * HEAD_DIM
    gn = N_GROUPS * D_STATE
    a_rows = jnp.broadcast_to(a_log.astype(F32)[:, :, None], (2, nh, CHUNK))
    d_row = jnp.repeat(d_skip.astype(F32), HEAD_DIM)
    in_tn = w_in.shape[-1] // 9

    def project(v, mod, width):
        n = v.shape[0]
        zx = matmul(v, w_in, w_index=k, mod=(norm1, mod[1], mod[0]), tm=_row_tile(n, 1024), tn=in_tn,
                    out_dtype=BF16, name="ssd_in")
        xbc = conv_silu(zx, conv_w, conv_b, width=width, col0=d_inner, tb=_row_tile(n, 1024))
        dt_t = dt_rows(zx, dt_bias, col0=2 * d_inner + 2 * gn, tb=_row_tile(n, 512))
        return zx, xbc, dt_t

    zx_c, xbc_c, dt_c = project(ctx_s, mod_c, ctx_s.shape[0])
    zx_l, xbc_l, dt_l = project(x, mod_l, GRID_W)
    zero = jnp.zeros((nh // 2, 2 * D_STATE, V7X_LANES), F32)
    yc, s_f = ssd_scan_dir(xbc_c, dt_c, a_rows[0], zero, None, direction=0)
    yl, _ = ssd_scan_dir(xbc_l, dt_l, a_rows[0], s_f, None, direction=0)
    yc, s_b = ssd_scan_dir(xbc_c, dt_c, a_rows[1], zero, yc, direction=1,
                           gate=(zx_c, d_row, norm_w) if ctx_out else None)
    yl, _ = ssd_scan_dir(xbc_l, dt_l, a_rows[1], s_b, yl, direction=1, gate=(zx_l, d_row, norm_w))

    def output(v, yn, gate):
        return matmul(yn, w_out, w_index=k, res=v, gate=gate, tm=_row_tile(v.shape[0], 512), tn=1024,
                      w_outer=True, name="ssd_out")

    x_new = output(x, yl, mod_l[2])
    ctx_new = output(ctx_s, yc, mod_c[2]) if ctx_out else None
    return x_new, ctx_new


def _dft_tables(n):
    j = np.arange(n, dtype=np.int64)
    ang = ((j[:, None] * j[None, :]) % n).astype(np.float64) * (2.0 * math.pi / n)
    return np.cos(ang), np.sin(ang)


def _fchan_kernel(x_ref, gain_ref, sc_ref, sh_ref, cs_ref, o_ref):
    h = _rms_mod(x_ref[...], gain_ref[...], sc_ref[...], sh_ref[...]).astype(BF16)
    gw = cs_ref.shape[0]
    d = h.shape[1]
    for g in range(d // gw):
        r = jnp.dot(h[:, g * gw:(g + 1) * gw], cs_ref[...], preferred_element_type=F32)
        o_ref[:, g * gw:(g + 1) * gw] = r[:, :gw]
        o_ref[:, d + g * gw:d + (g + 1) * gw] = r[:, gw:]


def _f1_kernel(r_ref, i_ref, k1_ref, tc_ref, ts_ref, yr_ref, yi_ref):
    la, sub, cw = r_ref.shape
    rows = la * sub
    xx = jnp.concatenate([r_ref[...].reshape(rows, cw), i_ref[...].reshape(rows, cw)], axis=0).astype(BF16)
    y = jnp.dot(k1_ref[...], xx, preferred_element_type=F32)
    yr, yi = y[:rows], y[rows:]
    c, s = tc_ref[...], ts_ref[...]
    yr_ref[...] = (c * yr + s * yi).reshape(la, sub, cw)
    yi_ref[...] = (c * yi - s * yr).reshape(la, sub, cw)


def _f2_kernel(r_ref, i_ref, k2_ref, o_ref):
    sub, lb, cw = r_ref.shape
    xx = jnp.concatenate([r_ref[...].reshape(sub * lb, cw), i_ref[...].reshape(sub * lb, cw)],
                         axis=0).astype(BF16)
    o_ref[...] = jnp.dot(k2_ref[...], xx, preferred_element_type=F32).reshape(lb, sub, cw)


def fnet_layer(x, mod, norm1, w_out, k, *, cw=512):
    l, d = x.shape
    sub = V7X_SUBLANES
    la = l // 128 if l >= 1024 else sub
    lb = l // la
    gw = d // FNET_GROUPS
    assert la % sub == 0 and lb % sub == 0 and d % cw == 0
    shift, scale, gate = mod
    cc, sc = _dft_tables(gw)
    norm = 1.0 / math.sqrt(l * gw)
    cs = jnp.asarray(np.concatenate([cc, -sc], axis=1) * norm, dtype=BF16)
    tm = _row_tile(l, 512)
    vec = pl.BlockSpec((1, d), lambda i: (0, 0))
    g2 = pl.pallas_call(
        _fchan_kernel,
        grid=(l // tm,),
        in_specs=[pl.BlockSpec((tm, d), lambda i: (i, 0)), vec, vec, vec, pl.BlockSpec((gw, 2 * gw), lambda i: (0, 0))],
        out_specs=pl.BlockSpec((tm, 2 * d), lambda i: (i, 0)),
        out_shape=jax.ShapeDtypeStruct((l, 2 * d), F32),
        compiler_params=_params("arbitrary"),
        name="fnet_chan",
    )(x, norm1.reshape(1, d), scale.reshape(1, d), shift.reshape(1, d), cs)
    ca, sa = _dft_tables(la)
    eye = np.eye(sub)
    k1 = jnp.asarray(np.block([[np.kron(ca, eye), np.kron(sa, eye)], [np.kron(-sa, eye), np.kron(ca, eye)]]),
                     dtype=BF16)
    bi = np.arange(lb, dtype=np.int64).reshape(lb // sub, 1, sub)
    di = np.arange(la, dtype=np.int64).reshape(1, la, 1)
    ang = ((bi * di) % l).astype(np.float64) * (2.0 * math.pi / l)
    tc = jnp.asarray(np.cos(ang).reshape(lb // sub, la * sub, 1), dtype=F32)
    ts = jnp.asarray(np.sin(ang).reshape(lb // sub, la * sub, 1), dtype=F32)
    g3 = g2.reshape(la, lb, 2 * d)
    ncb = d // cw
    blk1 = lambda off: pl.BlockSpec((la, sub, cw), lambda b, c: (0, b, off + c))
    twb = pl.BlockSpec((None, la * sub, 1), lambda b, c: (b, 0, 0))
    yr3, yi3 = pl.pallas_call(
        _f1_kernel,
        grid=(lb // sub, ncb),
        in_specs=[blk1(0), blk1(ncb), pl.BlockSpec(k1.shape, lambda b, c: (0, 0)), twb, twb],
        out_specs=[blk1(0), blk1(0)],
        out_shape=[jax.ShapeDtypeStruct((la, lb, d), F32)] * 2,
        compiler_params=_params("arbitrary", "arbitrary"),
        name="fnet_stage1",
    )(g3, g3, k1, tc, ts)
    cb_, sb_ = _dft_tables(lb)
    k2 = jnp.asarray(np.concatenate([np.einsum("cb,de->cdeb", cb_, eye).reshape(lb * sub, sub * lb),
                                     np.einsum("cb,de->cdeb", sb_, eye).reshape(lb * sub, sub * lb)], axis=1),
                     dtype=BF16)
    blk2 = pl.BlockSpec((sub, lb, cw), lambda dd, c: (dd, 0, c))
    f3 = pl.pallas_call(
        _f2_kernel,
        grid=(la // sub, ncb),
        in_specs=[blk2, blk2, pl.BlockSpec(k2.shape, lambda dd, c: (0, 0))],
        out_specs=pl.BlockSpec((lb, sub, cw), lambda dd, c: (0, dd, c)),
        out_shape=jax.ShapeDtypeStruct((lb, la, d), F32),
        compiler_params=_params("arbitrary", "arbitrary"),
        name="fnet_stage2",
    )(yr3, yi3, k2)
    return matmul(f3.reshape(l, d), w_out, w_index=k, res=x, gate=gate, tm=_row_tile(l, 512), tn=d,
                  name="fnet_out")


ROUTE_S = 128
ROUTE_TOKENS = ROUTE_S * ROUTE_S


def _moe_prep_kernel(x_ref, gain_ref, sc_ref, sh_ref, wr_ref, h_ref, lg_ref):
    h = _rms_mod(x_ref[...], gain_ref[...], sc_ref[...], sh_ref[...])
    tb = h.shape[0]
    for j in range(TOK_ROWS):
        h_ref[pl.ds(j, tb, stride=TOK_ROWS), :] = h[:, j * V7X_LANES:(j + 1) * V7X_LANES]
    def split(v):
        hi = v.astype(BF16)
        return hi, (v - hi.astype(F32)).astype(BF16)

    nt = lambda a, b: lax.dot_general(a, b, (((1,), (1,)), ((), ())), preferred_element_type=F32)
    (w_hi, w_lo), (h_hi, h_lo) = split(wr_ref[...]), split(h)
    lg_ref[...] = nt(w_hi, h_hi) + (nt(w_lo, h_hi) + nt(w_hi, h_lo))


def moe_prep(x, norm2, scale, shift, w_router, *, tb):
    n, d = x.shape
    e = w_router.shape[1]
    assert d == TOK_ROWS * V7X_LANES and n % tb == 0
    vec = pl.BlockSpec((1, d), lambda i: (0, 0))
    return pl.pallas_call(
        _moe_prep_kernel,
        grid=(n // tb,),
        in_specs=[pl.BlockSpec((tb, d), lambda i: (i, 0)), vec, vec, vec, pl.BlockSpec((e, d), lambda i: (0, 0))],
        out_specs=[pl.BlockSpec((tb * TOK_ROWS, V7X_LANES), lambda i: (i, 0)), pl.BlockSpec((e, tb), lambda i: (0, i))],
        out_shape=[jax.ShapeDtypeStruct((n * TOK_ROWS, V7X_LANES), F32), jax.ShapeDtypeStruct((e, n), F32)],
        compiler_params=_params("arbitrary"),
        name="moe_prep",
    )(x, norm2.reshape(1, d), scale.reshape(1, d), shift.reshape(1, d), w_router.T)


def _tok_cumsum(m, lincl_bf16, ustrict):
    wc = jnp.dot(lincl_bf16, m.astype(BF16), preferred_element_type=F32)
    coltot = wc[ROUTE_S - 1:ROUTE_S, :]
    colpref = jnp.dot(jnp.broadcast_to(coltot, (V7X_SUBLANES, ROUTE_S)), ustrict, precision=HIGHEST,
                      preferred_element_type=F32)[0:1]
    return wc, coltot, colpref


def _route_kernel(lg_ref, idx_ref, q_ref, g_ref, off_ref, cnt_ref, aff_scr, bits_scr, sel_scr, qt_scr, *,
                  n_valid, cap, pc):
    ne = lg_ref.shape[0]
    s_ = ROUTE_S
    c_pad = idx_ref.shape[2]
    si = lax.broadcasted_iota(I32, (s_, s_), 0)
    ji = lax.broadcasted_iota(I32, (s_, s_), 1)
    valid = ji * s_ + si < n_valid
    lincl = (ji <= si).astype(BF16)
    ustrict = (si < ji).astype(F32)
    ones = jnp.ones((s_, s_), BF16)

    ls = [lg_ref[e] for e in range(ne)]
    mx = functools.reduce(jnp.maximum, ls)
    ex = [jnp.exp(l - mx) for l in ls]
    den = functools.reduce(jnp.add, ex)
    for e in range(ne):
        aff = jnp.where(valid, ex[e] / den, -1.0)
        aff_scr[e] = aff
        bits_scr[e] = pltpu.bitcast(aff, I32)

    def bit_step(i, ts):
        bit = jnp.left_shift(jnp.int32(1), 30 - i)
        out = []
        for e in range(ne):
            cand = ts[e] | bit
            ge = (bits_scr[e] >= cand).astype(F32)
            c1 = jnp.sum(jnp.sum(ge, axis=0, keepdims=True), axis=1, keepdims=True)
            out.append(jnp.where(c1 >= cap, cand, ts[e]))
        return tuple(out)

    ts = lax.fori_loop(0, 31, bit_step, tuple(jnp.zeros((1, 1), I32) for _ in range(ne)))

    cnt = jnp.zeros((s_, s_), F32)
    for e in range(ne):
        bits = bits_scr[e]
        gt = bits > ts[e]
        eq = (bits == ts[e]).astype(F32)
        n_gt = jnp.sum(jnp.sum(gt.astype(F32), axis=0, keepdims=True), axis=1, keepdims=True)
        wc, _, colpref = _tok_cumsum(eq, lincl, ustrict)
        rank_eq = wc + colpref - eq
        sel = jnp.where(gt | ((eq > 0.0) & (rank_eq < cap - n_gt)), 1.0, 0.0)
        sel_scr[e] = sel
        qt_scr[e] = cnt
        cnt = cnt + sel
    wc, _, colpref = _tok_cumsum(cnt, lincl, ustrict)
    offs = wc + colpref - cnt
    off_ref[...] = offs
    cnt_ref[...] = cnt

    def per_expert(e, carry):
        m = sel_scr[e]
        qt = qt_scr[e] + offs
        aff = aff_scr[e]
        wc, coltot, colpref = _tok_cumsum(m, lincl, ustrict)
        wc_b = wc.astype(BF16)
        colcum = jnp.dot((lincl.astype(F32) * coltot).astype(BF16), ones, preferred_element_type=F32)
        colpref8 = jnp.broadcast_to(colpref, (V7X_SUBLANES, s_))
        sub = lax.broadcasted_iota(I32, (s_, pc), 0).astype(F32)
        reps = pc // s_
        colcum_t = jnp.concatenate([colcum] * reps, axis=1) if reps > 1 else colcum
        for c0 in range(0, c_pad, pc):
            p = (lax.broadcasted_iota(I32, (s_, pc), 1) + c0).astype(F32)
            p_row = p[0:1]
            blk = jnp.sum((colcum_t <= p).astype(F32), axis=0, keepdims=True)
            oh_j = (sub == blk).astype(F32)
            colvec = jnp.dot(wc_b, oh_j.astype(BF16), preferred_element_type=F32)
            cp = jnp.dot(colpref8, oh_j, precision=HIGHEST, preferred_element_type=F32)[0:1]
            s_idx = jnp.sum((colvec <= p_row - cp).astype(F32), axis=0, keepdims=True)
            oh_s = (sub == s_idx).astype(F32)
            gv = jnp.sum(oh_s * jnp.dot(aff, oh_j, precision=HIGHEST, preferred_element_type=F32),
                         axis=0, keepdims=True)
            qv = jnp.sum(oh_s * jnp.dot(qt, oh_j, precision=HIGHEST, preferred_element_type=F32),
                         axis=0, keepdims=True)
            live = p_row < cap
            idx_ref[e, :, c0:c0 + pc] = jnp.where(live, blk * s_ + s_idx, 0.0).astype(I32)
            q_ref[e, :, c0:c0 + pc] = jnp.where(live, qv, 0.0).astype(I32)
            g_ref[e, :, c0:c0 + pc] = jnp.where(live, gv, 0.0)
        return carry

    lax.fori_loop(0, ne, per_expert, 0)


def moe_route(logits_t, *, cap):
    ne, n = logits_t.shape
    assert n <= ROUTE_TOKENS and cap <= n
    s_ = ROUTE_S
    c_pad = max(cap, s_)
    pc = min(c_pad, 2 * s_)
    assert c_pad % pc == 0
    lg = jnp.pad(logits_t, ((0, 0), (0, ROUTE_TOKENS - n))).reshape(ne, s_, s_).transpose(0, 2, 1)
    full = lambda shape: pl.BlockSpec(shape, lambda i: (0,) * len(shape))
    slot = jax.ShapeDtypeStruct((ne, 1, c_pad), I32)
    idx, q, g, offs, cnt = pl.pallas_call(
        functools.partial(_route_kernel, n_valid=n, cap=cap, pc=pc),
        grid=(1,),
        in_specs=[full((ne, s_, s_))],
        out_specs=[full((ne, 1, c_pad))] * 3 + [full((s_, s_))] * 2,
        out_shape=[slot, slot, jax.ShapeDtypeStruct((ne, 1, c_pad), F32),
                   jax.ShapeDtypeStruct((s_, s_), F32), jax.ShapeDtypeStruct((s_, s_), F32)],
        scratch_shapes=[pltpu.VMEM((ne, s_, s_), F32), pltpu.VMEM((ne, s_, s_), I32),
                        pltpu.VMEM((ne, s_, s_), F32), pltpu.VMEM((ne, s_, s_), F32)],
        compiler_params=_params("arbitrary"),
        name="moe_route",
    )(lg)
    tok = lambda a: a.T.reshape(-1)[:n]
    return idx[:, 0, :cap], q[:, 0, :cap], g[:, 0, :cap], tok(offs), tok(cnt)


def _ffn_kernel(idx_ref, q_ref, g_ref, h_hbm, wg_ref, wu_ref, wd_ref, z_hbm, ga, gb, sa, sb, sems):
    step = pl.program_id(0)
    last = pl.num_programs(0) - 1
    cbh = ga.shape[0] // TOK_ROWS
    total = idx_ref.shape[0]
    base = step * (2 * cbh)
    srows = lambda p: pl.ds(p * TOK_ROWS, TOK_ROWS)
    drows = lambda r: pl.ds(pl.multiple_of(r * TOK_ROWS, TOK_ROWS), TOK_ROWS)
    G_A, G_B, S_A, S_B = range(4)

    def gather_start(buf, sem, slot0):
        for p in range(cbh):
            pltpu.make_async_copy(h_hbm.at[drows(idx_ref[slot0 + p]), :], buf.at[srows(p), :],
                                  sems.at[sem]).start(priority=p % 2)

    def gather_wait(buf, sem):
        for p in range(cbh):
            pltpu.make_async_copy(h_hbm.at[srows(0), :], buf.at[srows(p), :], sems.at[sem]).wait()

    def scatter_start(buf, sem, slot0):
        for p in range(cbh):
            pltpu.make_async_copy(buf.at[srows(p), :], z_hbm.at[drows(q_ref[slot0 + p]), :],
                                  sems.at[sem]).start(priority=p % 2)

    def scatter_wait(buf, sem):
        for p in range(cbh):
            pltpu.make_async_copy(buf.at[srows(p), :], z_hbm.at[srows(0), :], sems.at[sem]).wait()

    def compute(gbuf, sbuf, g):
        x = jnp.concatenate([gbuf[pl.ds(j, cbh, stride=TOK_ROWS), :] for j in range(TOK_ROWS)],
                            axis=1).astype(BF16)
        a = jnp.dot(x, wg_ref[...], preferred_element_type=F32)
        u = jnp.dot(x, wu_ref[...], preferred_element_type=F32)
        y = jnp.dot((_silu(a) * u).astype(BF16), wd_ref[...], preferred_element_type=F32) * g
        for j in range(TOK_ROWS):
            sbuf[pl.ds(j, cbh, stride=TOK_ROWS), :] = y[:, j * V7X_LANES:(j + 1) * V7X_LANES]

    @pl.when(step == 0)
    def _():
        gather_start(ga, G_A, 0)

    gather_wait(ga, G_A)

    @pl.when(step > 0)
    def _():
        scatter_wait(sa, S_A)

    gather_start(gb, G_B, base + cbh)
    compute(ga, sa, g_ref[0:cbh, :])
    scatter_start(sa, S_A, base)
    gather_wait(gb, G_B)

    @pl.when(step > 0)
    def _():
        scatter_wait(sb, S_B)

    gather_start(ga, G_A, jnp.minimum(base + 2 * cbh, total - cbh))
    compute(gb, sb, g_ref[cbh:2 * cbh, :])
    scatter_start(sb, S_B, base + cbh)

    @pl.when(step == last)
    def _():
        gather_wait(ga, G_A)
        scatter_wait(sa, S_A)
        scatter_wait(sb, S_B)


def moe_ffn(h_rows, idx, q, g, w_gate, w_up, w_down, layer, *, cbh):
    ne, cap = idx.shape
    d, f = w_gate.shape[2:]
    sb_rows = 2 * cbh
    assert cap % sb_rows == 0
    spe = cap // sb_rows
    wmap = lambda s, *_: (layer, s // spe, 0, 0)
    buf = pltpu.VMEM((cbh * TOK_ROWS, V7X_LANES), F32)
    return pl.pallas_call(
        _ffn_kernel,
        grid_spec=pltpu.PrefetchScalarGridSpec(
            num_scalar_prefetch=2,
            grid=(ne * spe,),
            in_specs=[
                pl.BlockSpec((sb_rows, 1), lambda s, *_: (s, 0)),
                pl.BlockSpec(memory_space=pl.ANY),
                pl.BlockSpec((None, None, d, f), wmap),
                pl.BlockSpec((None, None, d, f), wmap),
                pl.BlockSpec((None, None, f, d), wmap),
            ],
            out_specs=pl.BlockSpec(memory_space=pl.ANY),
            scratch_shapes=[buf, buf, buf, buf, pltpu.SemaphoreType.DMA((4,))],
        ),
        out_shape=jax.ShapeDtypeStruct((ne * cap * TOK_ROWS, V7X_LANES), F32),
        compiler_params=_params("arbitrary"),
        name="moe_ffn",
    )(idx.reshape(-1), q.reshape(-1), g.reshape(ne * cap, 1), h_rows, w_gate, w_up, w_down)


def _combine_kernel(boff_ref, x_ref, gate_ref, off_ref, cnt_ref, z_hbm, *rest, ch, p_total, final):
    fw_ref = rest[0] if final else None
    o_ref, z0, z1, sems = rest[1:] if final else rest
    i = pl.program_id(0)
    start = boff_ref[i]
    end = boff_ref[i + 1]
    nch = (end - start + ch - 1) // ch
    lo = off_ref[...]
    hi = lo + cnt_ref[...]
    o_ref[...] = jnp.zeros(o_ref.shape, F32)
    zbufs = (z0, z1)

    def row_start(c, first_row=None):
        first_row = start if first_row is None else first_row
        return jnp.minimum(first_row + c * ch, p_total - ch)

    def copy(c, slot, first_row=None):
        src = z_hbm.at[pl.ds(pl.multiple_of(row_start(c, first_row) * TOK_ROWS, TOK_ROWS), ch * TOK_ROWS), :]
        return pltpu.make_async_copy(src, zbufs[slot], sems.at[slot])

    def process(c, slot):
        copy(c, slot).wait()

        @pl.when(c + 1 < nch)
        def _():
            copy(c + 1, 1 - slot).start()

        first = start + c * ch
        qabs = row_start(c) + lax.broadcasted_iota(I32, (1, ch), 1)
        qf = qabs.astype(F32)
        seg = ((qf >= lo) & (qf < hi) & (qabs >= first)).astype(BF16)
        zb = zbufs[slot]
        z = jnp.concatenate([zb[pl.ds(j, ch, stride=TOK_ROWS), :] for j in range(TOK_ROWS)], axis=1)
        o_ref[...] += jnp.dot(seg, z.astype(BF16), preferred_element_type=F32)

    @pl.when((i == 0) & (nch > 0))
    def _():
        copy(0, 0).start()

    def pair(k, carry):
        process(2 * k, 0)

        @pl.when(2 * k + 1 < nch)
        def _():
            process(2 * k + 1, 1)

        return carry

    lax.fori_loop(0, (nch + 1) // 2, pair, 0)

    @pl.when(i + 1 < pl.num_programs(0))
    def _():
        @pl.when(boff_ref[i + 2] > end)
        def _():
            copy(0, 0, first_row=end).start()

    out = x_ref[...] + gate_ref[...] * o_ref[...]
    if final:
        out = out * lax.rsqrt(jnp.mean(out * out, axis=-1, keepdims=True) + EPS) * fw_ref[...]
    o_ref[...] = out


def moe_combine(x, gate, z_rows, offs, cnt, *, tb, ch=256, final_w=None):
    n, d = x.shape
    p_total = z_rows.shape[0] // TOK_ROWS
    assert n % tb == 0 and p_total >= ch
    boff = jnp.concatenate([offs[::tb], jnp.full((1,), p_total, F32)]).astype(I32)
    col = pl.BlockSpec((tb, 1), lambda i, s: (i, 0))
    vec = pl.BlockSpec((1, d), lambda i, s: (0, 0))
    extra_in, extra_spec = ([final_w.reshape(1, d)], [vec]) if final_w is not None else ([], [])
    return pl.pallas_call(
        functools.partial(_combine_kernel, ch=ch, p_total=p_total, final=final_w is not None),
        grid_spec=pltpu.PrefetchScalarGridSpec(
            num_scalar_prefetch=1,
            grid=(n // tb,),
            in_specs=[pl.BlockSpec((tb, d), lambda i, s: (i, 0)), vec,
                      col, col, pl.BlockSpec(memory_space=pl.ANY)] + extra_spec,
            out_specs=pl.BlockSpec((tb, d), lambda i, s: (i, 0)),
            scratch_shapes=[pltpu.VMEM((ch * TOK_ROWS, V7X_LANES), F32), pltpu.VMEM((ch * TOK_ROWS, V7X_LANES), F32),
                            pltpu.SemaphoreType.DMA((2,))],
        ),
        out_shape=jax.ShapeDtypeStruct((n, d), F32),
        compiler_params=_params("arbitrary"),
        name="moe_combine",
    )(boff, x, gate.reshape(1, d), offs.reshape(n, 1), cnt.reshape(n, 1), z_rows, *extra_in)


def moe_layer(x, mod, norm2, w_router, w_gate, w_up, w_down, layer, final_w=None):
    n = x.shape[0]
    ne = w_router.shape[1]
    cap = (CAPACITY_FACTOR * n) // ne
    shift, scale, gate = mod
    h_rows, logits_t = moe_prep(x, norm2, scale, shift, w_router, tb=_row_tile(n, 256))
    idx, q, g, offs, cnt = moe_route(logits_t, cap=cap)
    z_rows = moe_ffn(h_rows, idx, q, g, w_gate, w_up, w_down, layer, cbh=min(cap // 2, 256))
    return moe_combine(x, gate, z_rows, offs, cnt, tb=_row_tile(n, 512), final_w=final_w)


def kernel(x, c, ctx, c_ctx, ada_w, ada_b, norm1_w, norm2_w, final_norm_w, ssd_w_in, ssd_conv_w, ssd_conv_b,
           ssd_dt_bias, ssd_a_log, ssd_d, ssd_norm_w, ssd_w_out, fnet_w_out, moe_w_router, moe_w_gate, moe_w_up,
           moe_w_down):
    depth = ada_w.shape[0]
    d = x.shape[-1]
    assert x.shape[0] == 1, "one sample per call"
    xs, cs = x[0], ctx[0]
    ada = ada_all(c, c_ctx, ada_w, ada_b)
    moe_w = (moe_w_gate.astype(BF16), moe_w_up.astype(BF16), moe_w_down.astype(BF16))
    w_in_b, w_out_b, w_fnet_b = ssd_w_in.astype(BF16), ssd_w_out.astype(BF16), fnet_w_out.astype(BF16)
    for i in range(depth):
        k = i // 2
        is_ssd = i % 2 == 0
        ctx_later = any(j % 2 == 0 for j in range(i + 1, depth))
        sh1, sc1, g1, sh2, sc2, g2 = (ada[i, 0, m * d:(m + 1) * d] for m in range(6))
        csh1, csc1, cg1, csh2, csc2, cg2 = (ada[i, 1, m * d:(m + 1) * d] for m in range(6))
        if is_ssd:
            xs, cs_new = ssd_layer(xs, cs, (sh1, sc1, g1), (csh1, csc1, cg1), norm1_w[i], w_in_b, ssd_conv_w[k],
                                   ssd_conv_b[k], ssd_dt_bias[k], ssd_a_log[k], ssd_d[k], ssd_norm_w[k],
                                   w_out_b, k, ctx_later)
        else:
            xs = fnet_layer(xs, (sh1, sc1, g1), norm1_w[i], w_fnet_b, k)
            cs_new = fnet_layer(cs, (csh1, csc1, cg1), norm1_w[i], w_fnet_b, k) if ctx_later else None
        xs = moe_layer(xs, (sh2, sc2, g2), norm2_w[i], moe_w_router[i], *moe_w, i,
                       final_w=final_norm_w if i == depth - 1 else None)
        if ctx_later:
            cs = moe_layer(cs_new, (csh2, csc2, cg2), norm2_w[i], moe_w_router[i], *moe_w, i)
    return xs[None]
```

```python
import functools
import math

import jax
import jax.numpy as jnp
import numpy as np
from jax import lax
from jax.experimental import pallas as pl
from jax.experimental.pallas import tpu as pltpu

F32 = jnp.float32
BF16 = jnp.bfloat16
I32 = jnp.int32
HIGHEST = lax.Precision.HIGHEST

V7X_LANES = 128
V7X_SUBLANES = 8
V7X_VMEM_BYTES = 64 * 1024 * 1024
VMEM_LIMIT = V7X_VMEM_BYTES - 8 * 1024 * 1024

GRID_W = 64
HEAD_DIM = 64
N_GROUPS = 8
D_STATE = 128
CHUNK = 128
FNET_GROUPS = 8
N_EXPERTS = 16
CAPACITY_FACTOR = 2
EPS = 1e-6
TOK_ROWS = 16


def _params(*sem):
    return pltpu.CompilerParams(dimension_semantics=sem, vmem_limit_bytes=VMEM_LIMIT)


def _rms_mod(x, gain, scale, shift):
    y = x * lax.rsqrt(jnp.mean(x * x, axis=-1, keepdims=True) + EPS)
    return y * gain * (1.0 + scale) + shift


def _silu(x):
    return x * jax.nn.sigmoid(x)


def _ada_kernel(c_ref, w_ref, b_ref, o_ref):
    w = w_ref[...]
    rows = []
    for r in range(2):
        s = _silu(c_ref[r])
        rows.append(jnp.sum(s * w, axis=0, keepdims=True) + b_ref[...])
    rows.append(jnp.zeros((V7X_SUBLANES - 2, w.shape[1]), F32))
    o_ref[...] = jnp.concatenate(rows, axis=0)


def ada_all(c, c_ctx, ada_w, ada_b):
    depth, k, n6 = ada_w.shape
    tn = n6 // 8
    cond = jnp.stack([c[0], c_ctx], axis=0)[:, :, None]
    out = pl.pallas_call(
        _ada_kernel,
        grid=(depth, n6 // tn),
        in_specs=[
            pl.BlockSpec((2, k, 1), lambda l, n: (0, 0, 0)),
            pl.BlockSpec((None, k, tn), lambda l, n: (l, 0, n)),
            pl.BlockSpec((None, 1, tn), lambda l, n: (l, 0, n)),
        ],
        out_specs=pl.BlockSpec((None, V7X_SUBLANES, tn), lambda l, n: (l, 0, n)),
        out_shape=jax.ShapeDtypeStruct((depth, V7X_SUBLANES, n6), F32),
        compiler_params=_params("arbitrary", "arbitrary"),
        name="ada",
    )(cond, ada_w, ada_b[:, None, :])
    return out


def _mm_kernel(*refs, prologue, epilogue):
    it = iter(refs)
    a_ref = next(it)
    if prologue:
        gain_ref, sc_ref, sh_ref = next(it), next(it), next(it)
    w_ref = next(it)
    if epilogue:
        res_ref, gate_ref = next(it), next(it)
    o_ref = next(it)
    a_scr = next(it, None)

    if a_scr is None:
        a_bf16 = a_ref[...]
    else:
        @pl.when(pl.program_id(1) == 0)
        def _():
            a = a_ref[...]
            if prologue:
                a = _rms_mod(a, gain_ref[...], sc_ref[...], sh_ref[...])
            a_scr[...] = a.astype(BF16)

        a_bf16 = a_scr[...]
    acc = jnp.dot(a_bf16, w_ref[...], preferred_element_type=F32)
    if epilogue:
        acc = res_ref[...] + gate_ref[...] * acc
    o_ref[...] = acc.astype(o_ref.dtype)


def matmul(a, w_bf16, *, w_index=None, mod=None, res=None, gate=None, tm, tn, out_dtype=F32, name="mm",
           w_outer=False):
    m, k = a.shape
    n = w_bf16.shape[-1]
    assert m % tm == 0 and n % tn == 0
    use_scr = mod is not None or a.dtype != BF16
    assert not (w_outer and use_scr)
    ij = (lambda g0, g1: (g1, g0)) if w_outer else (lambda g0, g1: (g0, g1))
    row = lambda g0, g1: (0, 0)
    ins, specs = [a], [pl.BlockSpec((tm, k), lambda g0, g1: (ij(g0, g1)[0], 0))]
    if mod is not None:
        for v in mod:
            ins.append(v.reshape(1, k))
            specs.append(pl.BlockSpec((1, k), row))
    ins.append(w_bf16)
    if w_index is None:
        specs.append(pl.BlockSpec((k, tn), lambda g0, g1: (0, ij(g0, g1)[1])))
    else:
        specs.append(pl.BlockSpec((None, k, tn), lambda g0, g1: (w_index, 0, ij(g0, g1)[1])))
    if res is not None:
        ins += [res, gate.reshape(1, n)]
        specs += [pl.BlockSpec((tm, tn), ij), pl.BlockSpec((1, tn), lambda g0, g1: (0, ij(g0, g1)[1]))]
    return pl.pallas_call(
        functools.partial(_mm_kernel, prologue=mod is not None, epilogue=res is not None),
        grid=(n // tn, m // tm) if w_outer else (m // tm, n // tn),
        in_specs=specs,
        out_specs=pl.BlockSpec((tm, tn), ij),
        out_shape=jax.ShapeDtypeStruct((m, n), out_dtype),
        scratch_shapes=[pltpu.VMEM((tm, k), BF16)] if use_scr else [],
        compiler_params=_params("arbitrary", "arbitrary"),
        name=name,
    )(*ins)


def _conv_kernel(prev_ref, cur_ref, next_ref, w_ref, b_ref, o_ref, *, width):
    i = pl.program_id(0)
    last = pl.num_programs(0) - 1
    cur = cur_ref[...]
    tb, cb = cur.shape
    prev = jnp.where(i > 0, prev_ref[...], 0.0)
    nxt = jnp.where(i < last, next_ref[...], 0.0)
    if tb > width:
        up = jnp.concatenate([prev, cur[: tb - width]], axis=0)
        dn = jnp.concatenate([cur[width:], nxt], axis=0)
    else:
        up, dn = prev, nxt
    w = w_ref[...]

    def tap(dw):
        return up * w[dw:dw + 1] + cur * w[3 + dw:4 + dw] + dn * w[6 + dw:7 + dw]

    col = lax.broadcasted_iota(I32, (tb, cb), 0) & (width - 1)
    left = jnp.where(col == 0, 0.0, pltpu.roll(tap(0), 1, axis=0))
    right = jnp.where(col == width - 1, 0.0, pltpu.roll(tap(2), tb - 1, axis=0))
    o_ref[...] = _silu(left + tap(1) + right + b_ref[...])


def conv_silu(zx, conv_w, conv_b, *, width, col0, tb, cb=512):
    n = zx.shape[0]
    c = conv_w.shape[-1]
    assert width & (width - 1) == 0 and tb % width == 0 and n % tb == 0 and c % cb == 0 and col0 % cb == 0
    cblk0 = col0 // cb
    per = tb // width
    nrow = n // width
    return pl.pallas_call(
        functools.partial(_conv_kernel, width=width),
        grid=(n // tb, c // cb),
        in_specs=[
            pl.BlockSpec((width, cb), lambda i, j: (jnp.maximum(i * per - 1, 0), cblk0 + j)),
            pl.BlockSpec((tb, cb), lambda i, j: (i, cblk0 + j)),
            pl.BlockSpec((width, cb), lambda i, j: (jnp.minimum((i + 1) * per, nrow - 1), cblk0 + j)),
            pl.BlockSpec((9, cb), lambda i, j: (0, j)),
            pl.BlockSpec((1, cb), lambda i, j: (0, j)),
        ],
        out_specs=pl.BlockSpec((tb, cb), lambda i, j: (i, j)),
        out_shape=jax.ShapeDtypeStruct((n, c), F32),
        compiler_params=_params("arbitrary", "arbitrary"),
        name="conv",
    )(zx, zx, zx, conv_w.reshape(9, c), conv_b.reshape(1, c))


def _dt_kernel(x_ref, b_ref, o_ref):
    v = x_ref[...] + b_ref[...]
    sp = jnp.maximum(v, 0.0) + jnp.log1p(jnp.exp(-jnp.abs(v)))
    o_ref[...] = sp.T


def dt_rows(zx, dt_bias, *, col0, tb):
    n = zx.shape[0]
    w = dt_bias.size
    assert col0 % w == 0 and n % tb == 0
    return pl.pallas_call(
        _dt_kernel,
        grid=(n // tb,),
        in_specs=[pl.BlockSpec((tb, w), lambda i: (i, col0 // w)), pl.BlockSpec((1, w), lambda i: (0, 0))],
        out_specs=pl.BlockSpec((w, tb), lambda i: (0, i)),
        out_shape=jax.ShapeDtypeStruct((w, n), F32),
        compiler_params=_params("arbitrary"),
        name="dt",
    )(zx, dt_bias.reshape(1, w))


def _scan_kernel(x_ref, b_ref, c_ref, dt_ref, a_ref, s0_ref, *rest, rev, has_prev, gated):
    rest = list(rest)
    yp_ref = rest.pop(0) if has_prev else None
    z_ref, dsk_ref, nw_ref = (rest.pop(0), rest.pop(0), rest.pop(0)) if gated else (None, None, None)
    y_ref, sf_ref, st, rows = rest
    step = pl.program_id(0)
    q = CHUNK
    nh = dt_ref.shape[0]
    hpg = nh // N_GROUPS
    half = V7X_LANES // 2

    @pl.when(step == 0)
    def _():
        st[...] = s0_ref[...]

    ri = lax.broadcasted_iota(I32, (q, q), 0)
    ci = lax.broadcasted_iota(I32, (q, q), 1)
    if rev:
        upper, keep = ri >= ci, ri <= ci
    else:
        upper, keep = ri <= ci, ri >= ci
    lm = keep.astype(F32)
    dt = dt_ref[...]
    da = dt * -jnp.exp(a_ref[...])
    acs = jnp.dot(da, upper.astype(F32), precision=HIGHEST, preferred_element_type=F32)
    tot = jnp.sum(da, axis=1, keepdims=True)
    rows[0] = dt
    rows[1] = da
    rows[2] = acs
    rows[3] = dt * jnp.exp(tot - acs)
    rows[4] = jnp.broadcast_to(jnp.exp(tot), (nh, q))

    lane_lo = lax.broadcasted_iota(I32, (q, V7X_LANES), 1) < half
    r2 = lax.broadcasted_iota(I32, (2 * D_STATE, V7X_LANES), 0) < D_STATE
    l2 = lax.broadcasted_iota(I32, (2 * D_STATE, V7X_LANES), 1) < half
    diag = r2 == l2

    def group(g, carry):
        r0 = pl.multiple_of(g * hpg, hpg)
        dtg = rows[0, pl.ds(r0, hpg), :]
        dag = rows[1, pl.ds(r0, hpg), :]
        acg = rows[2, pl.ds(r0, hpg), :]
        wtg = rows[3, pl.ds(r0, hpg), :]
        etg = rows[4, pl.ds(r0, hpg), :]
        c0 = pl.multiple_of(g * D_STATE, D_STATE)
        cg = c_ref[:, pl.ds(c0, D_STATE)]
        bg = b_ref[:, pl.ds(c0, D_STATE)]
        cb = lax.dot_general(cg.astype(BF16), bg.astype(BF16), (((1,), (1,)), ((), ())),
                             preferred_element_type=F32)
        bgt = bg.T
        yps = []
        for k in range(hpg // 2):
            x0 = pl.multiple_of(g * (hpg * HEAD_DIM) + k * V7X_LANES, V7X_LANES)
            x2 = x_ref[:, pl.ds(x0, V7X_LANES)]
            ms, ces, bws, ess = [], [], [], []
            for t in range(2):
                j = 2 * k + t
                acol = jnp.sum(lm * dag[j:j + 1, :], axis=1, keepdims=True)
                dec = jnp.where(keep, jnp.exp(acol - acg[j:j + 1, :]), 0.0)
                ms.append(cb * dec * dtg[j:j + 1, :])
                ces.append(cg * jnp.exp(acol))
                bws.append(bgt * wtg[j:j + 1, :])
                ess.append(jnp.broadcast_to(etg[j:j + 1, :], (D_STATE, V7X_LANES)))
            sp = st[g * (hpg // 2) + k]
            lhs = jnp.concatenate(ms + ces, axis=1).astype(BF16)
            rhs = jnp.concatenate([jnp.where(lane_lo, x2, 0.0), jnp.where(lane_lo, 0.0, x2), sp],
                                  axis=0).astype(BF16)
            yp = jnp.dot(lhs, rhs, preferred_element_type=F32)
            if has_prev:
                yp = yp + yp_ref[:, pl.ds(x0, V7X_LANES)]
            if gated:
                yps.append(yp)
            else:
                y_ref[:, pl.ds(x0, V7X_LANES)] = yp
            upd = jnp.dot(jnp.concatenate(bws, axis=0).astype(BF16), x2.astype(BF16),
                          preferred_element_type=F32)
            st[g * (hpg // 2) + k] = sp * jnp.concatenate(ess, axis=0) + jnp.where(diag, upd, 0.0)
        if gated:
            lanes = pl.ds(pl.multiple_of(g * (hpg * HEAD_DIM), hpg * HEAD_DIM), hpg * HEAD_DIM)
            v = (jnp.concatenate(yps, axis=1) + x_ref[:, lanes] * dsk_ref[:, lanes]) * _silu(z_ref[:, lanes])
            r = lax.rsqrt(jnp.mean(v * v, axis=-1, keepdims=True) + EPS)
            y_ref[:, lanes] = (v * r * nw_ref[:, lanes]).astype(y_ref.dtype)
        return carry

    lax.fori_loop(0, N_GROUPS, group, 0, unroll=True)

    @pl.when(step == pl.num_programs(0) - 1)
    def _():
        sf_ref[...] = st[...]


def ssd_scan_dir(xbc, dt_t, a_row, s0, y_prev, *, direction, gate=None):
    n = xbc.shape[0]
    nh = dt_t.shape[0] // 2
    d_inner = nh * HEAD_DIM
    gn = N_GROUPS * D_STATE
    assert n % CHUNK == 0 and d_inner % gn == 0
    nc = n // CHUNK
    rev = direction == 1
    cidx = (lambda i: nc - 1 - i) if rev else (lambda i: i)
    ins = [xbc, xbc, xbc, dt_t, a_row, s0]
    specs = [
        pl.BlockSpec((CHUNK, d_inner), lambda i: (cidx(i), 0)),
        pl.BlockSpec((CHUNK, gn), lambda i: (cidx(i), d_inner // gn)),
        pl.BlockSpec((CHUNK, gn), lambda i: (cidx(i), d_inner // gn + 1)),
        pl.BlockSpec((nh, CHUNK), lambda i: (direction, cidx(i))),
        pl.BlockSpec((nh, CHUNK), lambda i: (0, 0)),
        pl.BlockSpec(s0.shape, lambda i: (0, 0, 0)),
    ]
    if y_prev is not None:
        ins.append(y_prev)
        specs.append(pl.BlockSpec((CHUNK, d_inner), lambda i: (cidx(i), 0)))
    if gate is not None:
        zx, d_row, norm_w = gate
        vec = pl.BlockSpec((1, d_inner), lambda i: (0, 0))
        ins += [zx, d_row.reshape(1, d_inner), norm_w.reshape(1, d_inner)]
        specs += [pl.BlockSpec((CHUNK, d_inner), lambda i: (cidx(i), 0)), vec, vec]
    return pl.pallas_call(
        functools.partial(_scan_kernel, rev=rev, has_prev=y_prev is not None, gated=gate is not None),
        grid=(nc,),
        in_specs=specs,
        out_specs=[pl.BlockSpec((CHUNK, d_inner), lambda i: (cidx(i), 0)),
                   pl.BlockSpec(s0.shape, lambda i: (0, 0, 0))],
        out_shape=[jax.ShapeDtypeStruct((n, d_inner), F32 if gate is None else BF16),
                   jax.ShapeDtypeStruct(s0.shape, F32)],
        scratch_shapes=[pltpu.VMEM(s0.shape, F32), pltpu.VMEM((5, nh, CHUNK), F32)],
        compiler_params=_params("arbitrary"),
        name="scan_bwd" if rev else "scan_fwd",
    )(*ins)


def _row_tile(n, pref):
    return pref if n % pref == 0 else n


def ssd_layer(x, ctx_s, mod_l, mod_c, norm1, w_in, conv_w, conv_b, dt_bias, a_log, d_skip, norm_w, w_out, k, ctx_out):
    nh = a_log.shape[1]
    d_inner = nh * HEAD_DIM
    gn = N_GROUPS * D_STATE
    a_rows = jnp.broadcast_to(a_log.astype(F32)[:, :, None], (2, nh, CHUNK))
    d_row = jnp.repeat(d_skip.astype(F32), HEAD_DIM)
    in_tn = w_in.shape[-1] // 9

    def project(v, mod, width):
        n = v.shape[0]
        zx = matmul(v, w_in, w_index=k, mod=(norm1, mod[1], mod[0]), tm=_row_tile(n, 1024), tn=in_tn,
                    name="ssd_in")
        xbc = conv_silu(zx, conv_w, conv_b, width=width, col0=d_inner, tb=_row_tile(n, 1024))
        dt_t = dt_rows(zx, dt_bias, col0=2 * d_inner + 2 * gn, tb=_row_tile(n, 512))
        return zx, xbc, dt_t

    zx_c, xbc_c, dt_c = project(ctx_s, mod_c, ctx_s.shape[0])
    zx_l, xbc_l, dt_l = project(x, mod_l, GRID_W)
    zero = jnp.zeros((nh // 2, 2 * D_STATE, V7X_LANES), F32)
    yc, s_f = ssd_scan_dir(xbc_c, dt_c, a_rows[0], zero, None, direction=0)
    yl, _ = ssd_scan_dir(xbc_l, dt_l, a_rows[0], s_f, None, direction=0)
    yc, s_b = ssd_scan_dir(xbc_c, dt_c, a_rows[1], zero, yc, direction=1,
                           gate=(zx_c, d_row, norm_w) if ctx_out else None)
    yl, _ = ssd_scan_dir(xbc_l, dt_l, a_rows[1], s_b, yl, direction=1, gate=(zx_l, d_row, norm_w))

    def output(v, yn, gate):
        return matmul(yn, w_out, w_index=k, res=v, gate=gate, tm=_row_tile(v.shape[0], 512), tn=1024,
                      w_outer=True, name="ssd_out")

    x_new = output(x, yl, mod_l[2])
    ctx_new = output(ctx_s, yc, mod_c[2]) if ctx_out else None
    return x_new, ctx_new


def _dft_tables(n):
    j = np.arange(n, dtype=np.int64)
    ang = ((j[:, None] * j[None, :]) % n).astype(np.float64) * (2.0 * math.pi / n)
    return np.cos(ang), np.sin(ang)


def _fchan_kernel(x_ref, gain_ref, sc_ref, sh_ref, cs_ref, o_ref):
    h = _rms_mod(x_ref[...], gain_ref[...], sc_ref[...], sh_ref[...]).astype(BF16)
    gw = cs_ref.shape[0]
    d = h.shape[1]
    for g in range(d // gw):
        r = jnp.dot(h[:, g * gw:(g + 1) * gw], cs_ref[...], preferred_element_type=F32)
        o_ref[:, g * gw:(g + 1) * gw] = r[:, :gw]
        o_ref[:, d + g * gw:d + (g + 1) * gw] = r[:, gw:]


def _f1_kernel(r_ref, i_ref, k1_ref, tc_ref, ts_ref, yr_ref, yi_ref):
    la, sub, cw = r_ref.shape
    rows = la * sub
    xx = jnp.concatenate([r_ref[...].reshape(rows, cw), i_ref[...].reshape(rows, cw)], axis=0).astype(BF16)
    y = jnp.dot(k1_ref[...], xx, preferred_element_type=F32)
    yr, yi = y[:rows], y[rows:]
    c, s = tc_ref[...], ts_ref[...]
    yr_ref[...] = (c * yr + s * yi).reshape(la, sub, cw)
    yi_ref[...] = (c * yi - s * yr).reshape(la, sub, cw)


def _f2_kernel(r_ref, i_ref, k2_ref, o_ref):
    sub, lb, cw = r_ref.shape
    xx = jnp.concatenate([r_ref[...].reshape(sub * lb, cw), i_ref[...].reshape(sub * lb, cw)],
                         axis=0).astype(BF16)
    o_ref[...] = jnp.dot(k2_ref[...], xx, preferred_element_type=F32).reshape(lb, sub, cw)


def fnet_layer(x, mod, norm1, w_out, k, *, cw=512):
    l, d = x.shape
    sub = V7X_SUBLANES
    la = l // 128 if l >= 1024 else sub
    lb = l // la
    gw = d // FNET_GROUPS
    assert la % sub == 0 and lb % sub == 0 and d % cw == 0
    shift, scale, gate = mod
    cc, sc = _dft_tables(gw)
    norm = 1.0 / math.sqrt(l * gw)
    cs = jnp.asarray(np.concatenate([cc, -sc], axis=1) * norm, dtype=BF16)
    tm = _row_tile(l, 512)
    vec = pl.BlockSpec((1, d), lambda i: (0, 0))
    g2 = pl.pallas_call(
        _fchan_kernel,
        grid=(l // tm,),
        in_specs=[pl.BlockSpec((tm, d), lambda i: (i, 0)), vec, vec, vec, pl.BlockSpec((gw, 2 * gw), lambda i: (0, 0))],
        out_specs=pl.BlockSpec((tm, 2 * d), lambda i: (i, 0)),
        out_shape=jax.ShapeDtypeStruct((l, 2 * d), F32),
        compiler_params=_params("arbitrary"),
        name="fnet_chan",
    )(x, norm1.reshape(1, d), scale.reshape(1, d), shift.reshape(1, d), cs)
    ca, sa = _dft_tables(la)
    eye = np.eye(sub)
    k1 = jnp.asarray(np.block([[np.kron(ca, eye), np.kron(sa, eye)], [np.kron(-sa, eye), np.kron(ca, eye)]]),
                     dtype=BF16)
    bi = np.arange(lb, dtype=np.int64).reshape(lb // sub, 1, sub)
    di = np.arange(la, dtype=np.int64).reshape(1, la, 1)
    ang = ((bi * di) % l).astype(np.float64) * (2.0 * math.pi / l)
    tc = jnp.asarray(np.cos(ang).reshape(lb // sub, la * sub, 1), dtype=F32)
    ts = jnp.asarray(np.sin(ang).reshape(lb // sub, la * sub, 1), dtype=F32)
    g3 = g2.reshape(la, lb, 2 * d)
    ncb = d // cw
    blk1 = lambda off: pl.BlockSpec((la, sub, cw), lambda b, c: (0, b, off + c))
    twb = pl.BlockSpec((None, la * sub, 1), lambda b, c: (b, 0, 0))
    yr3, yi3 = pl.pallas_call(
        _f1_kernel,
        grid=(lb // sub, ncb),
        in_specs=[blk1(0), blk1(ncb), pl.BlockSpec(k1.shape, lambda b, c: (0, 0)), twb, twb],
        out_specs=[blk1(0), blk1(0)],
        out_shape=[jax.ShapeDtypeStruct((la, lb, d), F32)] * 2,
        compiler_params=_params("arbitrary", "arbitrary"),
        name="fnet_stage1",
    )(g3, g3, k1, tc, ts)
    cb_, sb_ = _dft_tables(lb)
    k2 = jnp.asarray(np.concatenate([np.einsum("cb,de->cdeb", cb_, eye).reshape(lb * sub, sub * lb),
                                     np.einsum("cb,de->cdeb", sb_, eye).reshape(lb * sub, sub * lb)], axis=1),
                     dtype=BF16)
    blk2 = pl.BlockSpec((sub, lb, cw), lambda dd, c: (dd, 0, c))
    f3 = pl.pallas_call(
        _f2_kernel,
        grid=(la // sub, ncb),
        in_specs=[blk2, blk2, pl.BlockSpec(k2.shape, lambda dd, c: (0, 0))],
        out_specs=pl.BlockSpec((lb, sub, cw), lambda dd, c: (0, dd, c)),
        out_shape=jax.ShapeDtypeStruct((lb, la, d), F32),
        compiler_params=_params("arbitrary", "arbitrary"),
        name="fnet_stage2",
    )(yr3, yi3, k2)
    return matmul(f3.reshape(l, d), w_out, w_index=k, res=x, gate=gate, tm=_row_tile(l, 512), tn=d,
                  name="fnet_out")


ROUTE_S = 128
ROUTE_TOKENS = ROUTE_S * ROUTE_S


def _moe_prep_kernel(x_ref, gain_ref, sc_ref, sh_ref, wr_ref, h_ref, lg_ref):
    h = _rms_mod(x_ref[...], gain_ref[...], sc_ref[...], sh_ref[...])
    tb = h.shape[0]
    for j in range(TOK_ROWS):
        h_ref[pl.ds(j, tb, stride=TOK_ROWS), :] = h[:, j * V7X_LANES:(j + 1) * V7X_LANES]
    def split(v):
        hi = v.astype(BF16)
        return hi, (v - hi.astype(F32)).astype(BF16)

    nt = lambda a, b: lax.dot_general(a, b, (((1,), (1,)), ((), ())), preferred_element_type=F32)
    (w_hi, w_lo), (h_hi, h_lo) = split(wr_ref[...]), split(h)
    lg_ref[...] = nt(w_hi, h_hi) + (nt(w_lo, h_hi) + nt(w_hi, h_lo))


def moe_prep(x, norm2, scale, shift, w_router, *, tb):
    n, d = x.shape
    e = w_router.shape[1]
    assert d == TOK_ROWS * V7X_LANES and n % tb == 0
    vec = pl.BlockSpec((1, d), lambda i: (0, 0))
    return pl.pallas_call(
        _moe_prep_kernel,
        grid=(n // tb,),
        in_specs=[pl.BlockSpec((tb, d), lambda i: (i, 0)), vec, vec, vec, pl.BlockSpec((e, d), lambda i: (0, 0))],
        out_specs=[pl.BlockSpec((tb * TOK_ROWS, V7X_LANES), lambda i: (i, 0)), pl.BlockSpec((e, tb), lambda i: (0, i))],
        out_shape=[jax.ShapeDtypeStruct((n * TOK_ROWS, V7X_LANES), F32), jax.ShapeDtypeStruct((e, n), F32)],
        compiler_params=_params("arbitrary"),
        name="moe_prep",
    )(x, norm2.reshape(1, d), scale.reshape(1, d), shift.reshape(1, d), w_router.T)


def _tok_cumsum(m, lincl_bf16, ustrict):
    wc = jnp.dot(lincl_bf16, m.astype(BF16), preferred_element_type=F32)
    coltot = wc[ROUTE_S - 1:ROUTE_S, :]
    colpref = jnp.dot(jnp.broadcast_to(coltot, (V7X_SUBLANES, ROUTE_S)), ustrict, precision=HIGHEST,
                      preferred_element_type=F32)[0:1]
    return wc, coltot, colpref


def _route_kernel(lg_ref, idx_ref, q_ref, g_ref, off_ref, cnt_ref, aff_scr, bits_scr, sel_scr, qt_scr, *,
                  n_valid, cap, pc):
    ne = lg_ref.shape[0]
    s_ = ROUTE_S
    c_pad = idx_ref.shape[2]
    si = lax.broadcasted_iota(I32, (s_, s_), 0)
    ji = lax.broadcasted_iota(I32, (s_, s_), 1)
    valid = ji * s_ + si < n_valid
    lincl = (ji <= si).astype(BF16)
    ustrict = (si < ji).astype(F32)
    ones = jnp.ones((s_, s_), BF16)

    ls = [lg_ref[e] for e in range(ne)]
    mx = functools.reduce(jnp.maximum, ls)
    ex = [jnp.exp(l - mx) for l in ls]
    den = functools.reduce(jnp.add, ex)
    for e in range(ne):
        aff = jnp.where(valid, ex[e] / den, -1.0)
        aff_scr[e] = aff
        bits_scr[e] = pltpu.bitcast(aff, I32)

    def bit_step(i, ts):
        bit = jnp.left_shift(jnp.int32(1), 30 - i)
        out = []
        for e in range(ne):
            cand = ts[e] | bit
            ge = (bits_scr[e] >= cand).astype(F32)
            c1 = jnp.sum(jnp.sum(ge, axis=0, keepdims=True), axis=1, keepdims=True)
            out.append(jnp.where(c1 >= cap, cand, ts[e]))
        return tuple(out)

    ts = lax.fori_loop(0, 31, bit_step, tuple(jnp.zeros((1, 1), I32) for _ in range(ne)))

    cnt = jnp.zeros((s_, s_), F32)
    for e in range(ne):
        bits = bits_scr[e]
        gt = bits > ts[e]
        eq = (bits == ts[e]).astype(F32)
        n_gt = jnp.sum(jnp.sum(gt.astype(F32), axis=0, keepdims=True), axis=1, keepdims=True)
        wc, _, colpref = _tok_cumsum(eq, lincl, ustrict)
        rank_eq = wc + colpref - eq
        sel = jnp.where(gt | ((eq > 0.0) & (rank_eq < cap - n_gt)), 1.0, 0.0)
        sel_scr[e] = sel
        qt_scr[e] = cnt
        cnt = cnt + sel
    wc, _, colpref = _tok_cumsum(cnt, lincl, ustrict)
    offs = wc + colpref - cnt
    off_ref[...] = offs
    cnt_ref[...] = cnt

    def per_expert(e, carry):
        m = sel_scr[e]
        qt = qt_scr[e] + offs
        aff = aff_scr[e]
        wc, coltot, colpref = _tok_cumsum(m, lincl, ustrict)
        wc_b = wc.astype(BF16)
        colcum = jnp.dot((lincl.astype(F32) * coltot).astype(BF16), ones, preferred_element_type=F32)
        colpref8 = jnp.broadcast_to(colpref, (V7X_SUBLANES, s_))
        sub = lax.broadcasted_iota(I32, (s_, pc), 0).astype(F32)
        reps = pc // s_
        colcum_t = jnp.concatenate([colcum] * reps, axis=1) if reps > 1 else colcum
        for c0 in range(0, c_pad, pc):
            p = (lax.broadcasted_iota(I32, (s_, pc), 1) + c0).astype(F32)
            p_row = p[0:1]
            blk = jnp.sum((colcum_t <= p).astype(F32), axis=0, keepdims=True)
            oh_j = (sub == blk).astype(F32)
            colvec = jnp.dot(wc_b, oh_j.astype(BF16), preferred_element_type=F32)
            cp = jnp.dot(colpref8, oh_j, precision=HIGHEST, preferred_element_type=F32)[0:1]
            s_idx = jnp.sum((colvec <= p_row - cp).astype(F32), axis=0, keepdims=True)
            oh_s = (sub == s_idx).astype(F32)
            gv = jnp.sum(oh_s * jnp.dot(aff, oh_j, precision=HIGHEST, preferred_element_type=F32),
                         axis=0, keepdims=True)
            qv = jnp.sum(oh_s * jnp.dot(qt, oh_j, precision=HIGHEST, preferred_element_type=F32),
                         axis=0, keepdims=True)
            live = p_row < cap
            idx_ref[e, :, c0:c0 + pc] = jnp.where(live, blk * s_ + s_idx, 0.0).astype(I32)
            q_ref[e, :, c0:c0 + pc] = jnp.where(live, qv, 0.0).astype(I32)
            g_ref[e, :, c0:c0 + pc] = jnp.where(live, gv, 0.0)
        return carry

    lax.fori_loop(0, ne, per_expert, 0)


def moe_route(logits_t, *, cap):
    ne, n = logits_t.shape
    assert n <= ROUTE_TOKENS and cap <= n
    s_ = ROUTE_S
    c_pad = max(cap, s_)
    pc = min(c_pad, 2 * s_)
    assert c_pad % pc == 0
    lg = jnp.pad(logits_t, ((0, 0), (0, ROUTE_TOKENS - n))).reshape(ne, s_, s_).transpose(0, 2, 1)
    full = lambda shape: pl.BlockSpec(shape, lambda i: (0,) * len(shape))
    slot = jax.ShapeDtypeStruct((ne, 1, c_pad), I32)
    idx, q, g, offs, cnt = pl.pallas_call(
        functools.partial(_route_kernel, n_valid=n, cap=cap, pc=pc),
        grid=(1,),
        in_specs=[full((ne, s_, s_))],
        out_specs=[full((ne, 1, c_pad))] * 3 + [full((s_, s_))] * 2,
        out_shape=[slot, slot, jax.ShapeDtypeStruct((ne, 1, c_pad), F32),
                   jax.ShapeDtypeStruct((s_, s_), F32), jax.ShapeDtypeStruct((s_, s_), F32)],
        scratch_shapes=[pltpu.VMEM((ne, s_, s_), F32), pltpu.VMEM((ne, s_, s_), I32),
                        pltpu.VMEM((ne, s_, s_), F32), pltpu.VMEM((ne, s_, s_), F32)],
        compiler_params=_params("arbitrary"),
        name="moe_route",
    )(lg)
    tok = lambda a: a.T.reshape(-1)[:n]
    return idx[:, 0, :cap], q[:, 0, :cap], g[:, 0, :cap], tok(offs), tok(cnt)


def _ffn_kernel(idx_ref, q_ref, g_ref, h_hbm, wg_ref, wu_ref, wd_ref, z_hbm, ga, gb, sa, sb, sems):
    step = pl.program_id(0)
    last = pl.num_programs(0) - 1
    cbh = ga.shape[0] // TOK_ROWS
    total = idx_ref.shape[0]
    base = step * (2 * cbh)
    srows = lambda p: pl.ds(p * TOK_ROWS, TOK_ROWS)
    drows = lambda r: pl.ds(pl.multiple_of(r * TOK_ROWS, TOK_ROWS), TOK_ROWS)
    G_A, G_B, S_A, S_B = range(4)

    def gather_start(buf, sem, slot0):
        for p in range(cbh):
            pltpu.make_async_copy(h_hbm.at[drows(idx_ref[slot0 + p]), :], buf.at[srows(p), :],
                                  sems.at[sem]).start(priority=p % 2)

    def gather_wait(buf, sem):
        for p in range(cbh):
            pltpu.make_async_copy(h_hbm.at[srows(0), :], buf.at[srows(p), :], sems.at[sem]).wait()

    def scatter_start(buf, sem, slot0):
        for p in range(cbh):
            pltpu.make_async_copy(buf.at[srows(p), :], z_hbm.at[drows(q_ref[slot0 + p]), :],
                                  sems.at[sem]).start(priority=p % 2)

    def scatter_wait(buf, sem):
        for p in range(cbh):
            pltpu.make_async_copy(buf.at[srows(p), :], z_hbm.at[srows(0), :], sems.at[sem]).wait()

    def compute(gbuf, sbuf, g):
        x = jnp.concatenate([gbuf[pl.ds(j, cbh, stride=TOK_ROWS), :] for j in range(TOK_ROWS)],
                            axis=1).astype(BF16)
        a = jnp.dot(x, wg_ref[...], preferred_element_type=F32)
        u = jnp.dot(x, wu_ref[...], preferred_element_type=F32)
        y = jnp.dot((_silu(a) * u).astype(BF16), wd_ref[...], preferred_element_type=F32) * g
        for j in range(TOK_ROWS):
            sbuf[pl.ds(j, cbh, stride=TOK_ROWS), :] = y[:, j * V7X_LANES:(j + 1) * V7X_LANES]

    @pl.when(step == 0)
    def _():
        gather_start(ga, G_A, 0)

    gather_wait(ga, G_A)

    @pl.when(step > 0)
    def _():
        scatter_wait(sa, S_A)

    gather_start(gb, G_B, base + cbh)
    compute(ga, sa, g_ref[0:cbh, :])
    scatter_start(sa, S_A, base)
    gather_wait(gb, G_B)

    @pl.when(step > 0)
    def _():
        scatter_wait(sb, S_B)

    gather_start(ga, G_A, jnp.minimum(base + 2 * cbh, total - cbh))
    compute(gb, sb, g_ref[cbh:2 * cbh, :])
    scatter_start(sb, S_B, base + cbh)

    @pl.when(step == last)
    def _():
        gather_wait(ga, G_A)
        scatter_wait(sa, S_A)
        scatter_wait(sb, S_B)


def moe_ffn(h_rows, idx, q, g, w_gate, w_up, w_down, layer, *, cbh):
    ne, cap = idx.shape
    d, f = w_gate.shape[2:]
    sb_rows = 2 * cbh
    assert cap % sb_rows == 0
    spe = cap // sb_rows
    wmap = lambda s, *_: (layer, s // spe, 0, 0)
    buf = pltpu.VMEM((cbh * TOK_ROWS, V7X_LANES), F32)
    return pl.pallas_call(
        _ffn_kernel,
        grid_spec=pltpu.PrefetchScalarGridSpec(
            num_scalar_prefetch=2,
            grid=(ne * spe,),
            in_specs=[
                pl.BlockSpec((sb_rows, 1), lambda s, *_: (s, 0)),
                pl.BlockSpec(memory_space=pl.ANY),
                pl.BlockSpec((None, None, d, f), wmap),
                pl.BlockSpec((None, None, d, f), wmap),
                pl.BlockSpec((None, None, f, d), wmap),
            ],
            out_specs=pl.BlockSpec(memory_space=pl.ANY),
            scratch_shapes=[buf, buf, buf, buf, pltpu.SemaphoreType.DMA((4,))],
        ),
        out_shape=jax.ShapeDtypeStruct((ne * cap * TOK_ROWS, V7X_LANES), F32),
        compiler_params=_params("arbitrary"),
        name="moe_ffn",
    )(idx.reshape(-1), q.reshape(-1), g.reshape(ne * cap, 1), h_rows, w_gate, w_up, w_down)


def _combine_kernel(boff_ref, x_ref, gate_ref, off_ref, cnt_ref, z_hbm, *rest, ch, p_total, final):
    fw_ref = rest[0] if final else None
    o_ref, z0, z1, sems = rest[1:] if final else rest
    i = pl.program_id(0)
    start = boff_ref[i]
    end = boff_ref[i + 1]
    nch = (end - start + ch - 1) // ch
    lo = off_ref[...]
    hi = lo + cnt_ref[...]
    o_ref[...] = jnp.zeros(o_ref.shape, F32)
    zbufs = (z0, z1)

    def row_start(c, first_row=None):
        first_row = start if first_row is None else first_row
        return jnp.minimum(first_row + c * ch, p_total - ch)

    def copy(c, slot, first_row=None):
        src = z_hbm.at[pl.ds(pl.multiple_of(row_start(c, first_row) * TOK_ROWS, TOK_ROWS), ch * TOK_ROWS), :]
        return pltpu.make_async_copy(src, zbufs[slot], sems.at[slot])

    def process(c, slot):
        copy(c, slot).wait()

        @pl.when(c + 1 < nch)
        def _():
            copy(c + 1, 1 - slot).start()

        first = start + c * ch
        qabs = row_start(c) + lax.broadcasted_iota(I32, (1, ch), 1)
        qf = qabs.astype(F32)
        seg = ((qf >= lo) & (qf < hi) & (qabs >= first)).astype(BF16)
        zb = zbufs[slot]
        z = jnp.concatenate([zb[pl.ds(j, ch, stride=TOK_ROWS), :] for j in range(TOK_ROWS)], axis=1)
        o_ref[...] += jnp.dot(seg, z.astype(BF16), preferred_element_type=F32)

    @pl.when((i == 0) & (nch > 0))
    def _():
        copy(0, 0).start()

    def pair(k, carry):
        process(2 * k, 0)

        @pl.when(2 * k + 1 < nch)
        def _():
            process(2 * k + 1, 1)

        return carry

    lax.fori_loop(0, (nch + 1) // 2, pair, 0)

    @pl.when(i + 1 < pl.num_programs(0))
    def _():
        @pl.when(boff_ref[i + 2] > end)
        def _():
            copy(0, 0, first_row=end).start()

    out = x_ref[...] + gate_ref[...] * o_ref[...]
    if final:
        out = out * lax.rsqrt(jnp.mean(out * out, axis=-1, keepdims=True) + EPS) * fw_ref[...]
    o_ref[...] = out


def moe_combine(x, gate, z_rows, offs, cnt, *, tb, ch=512, final_w=None):
    n, d = x.shape
    p_total = z_rows.shape[0] // TOK_ROWS
    assert n % tb == 0 and p_total >= ch
    boff = jnp.concatenate([offs[::tb], jnp.full((1,), p_total, F32)]).astype(I32)
    col = pl.BlockSpec((tb, 1), lambda i, s: (i, 0))
    vec = pl.BlockSpec((1, d), lambda i, s: (0, 0))
    extra_in, extra_spec = ([final_w.reshape(1, d)], [vec]) if final_w is not None else ([], [])
    return pl.pallas_call(
        functools.partial(_combine_kernel, ch=ch, p_total=p_total, final=final_w is not None),
        grid_spec=pltpu.PrefetchScalarGridSpec(
            num_scalar_prefetch=1,
            grid=(n // tb,),
            in_specs=[pl.BlockSpec((tb, d), lambda i, s: (i, 0)), vec,
                      col, col, pl.BlockSpec(memory_space=pl.ANY)] + extra_spec,
            out_specs=pl.BlockSpec((tb, d), lambda i, s: (i, 0)),
            scratch_shapes=[pltpu.VMEM((ch * TOK_ROWS, V7X_LANES), F32), pltpu.VMEM((ch * TOK_ROWS, V7X_LANES), F32),
                            pltpu.SemaphoreType.DMA((2,))],
        ),
        out_shape=jax.ShapeDtypeStruct((n, d), F32),
        compiler_params=_params("arbitrary"),
        name="moe_combine",
    )(boff, x, gate.reshape(1, d), offs.reshape(n, 1), cnt.reshape(n, 1), z_rows, *extra_in)


def moe_layer(x, mod, norm2, w_router, w_gate, w_up, w_down, layer, final_w=None):
    n = x.shape[0]
    ne = w_router.shape[1]
    cap = (CAPACITY_FACTOR * n) // ne
    shift, scale, gate = mod
    h_rows, logits_t = moe_prep(x, norm2, scale, shift, w_router, tb=_row_tile(n, 256))
    idx, q, g, offs, cnt = moe_route(logits_t, cap=cap)
    z_rows = moe_ffn(h_rows, idx, q, g, w_gate, w_up, w_down, layer, cbh=min(cap // 2, 256))
    return moe_combine(x, gate, z_rows, offs, cnt, tb=_row_tile(n, 1024), final_w=final_w)


def kernel(x, c, ctx, c_ctx, ada_w, ada_b, norm1_w, norm2_w, final_norm_w, ssd_w_in, ssd_conv_w, ssd_conv_b,
           ssd_dt_bias, ssd_a_log, ssd_d, ssd_norm_w, ssd_w_out, fnet_w_out, moe_w_router, moe_w_gate, moe_w_up,
           moe_w_down):
    depth = ada_w.shape[0]
    d = x.shape[-1]
    assert x.shape[0] == 1, "one sample per call"
    xs, cs = x[0], ctx[0]
    ada = ada_all(c, c_ctx, ada_w, ada_b)
    moe_w = (moe_w_gate.astype(BF16), moe_w_up.astype(BF16), moe_w_down.astype(BF16))
    w_in_b, w_out_b, w_fnet_b = ssd_w_in.astype(BF16), ssd_w_out.astype(BF16), fnet_w_out.astype(BF16)
    for i in range(depth):
        k = i // 2
        is_ssd = i % 2 == 0
        ctx_later = any(j % 2 == 0 for j in range(i + 1, depth))
        sh1, sc1, g1, sh2, sc2, g2 = (ada[i, 0, m * d:(m + 1) * d] for m in range(6))
        csh1, csc1, cg1, csh2, csc2, cg2 = (ada[i, 1, m * d:(m + 1) * d] for m in range(6))
        if is_ssd:
            xs, cs_new = ssd_layer(xs, cs, (sh1, sc1, g1), (csh1, csc1, cg1), norm1_w[i], w_in_b, ssd_conv_w[k],
                                   ssd_conv_b[k], ssd_dt_bias[k], ssd_a_log[k], ssd_d[k], ssd_norm_w[k],
                                   w_out_b, k, ctx_later)
        else:
            xs = fnet_layer(xs, (sh1, sc1, g1), norm1_w[i], w_fnet_b, k)
            cs_new = fnet_layer(cs, (csh1, csc1, cg1), norm1_w[i], w_fnet_b, k) if ctx_later else None
        xs = moe_layer(xs, (sh2, sc2, g2), norm2_w[i], moe_w_router[i], *moe_w, i,
                       final_w=final_norm_w if i == depth - 1 else None)
        if ctx_later:
            cs = moe_layer(cs_new, (csh2, csc2, cg2), norm2_w[i], moe_w_router[i], *moe_w, i)
    return xs[None]
```
